```python
import math
import jax
import jax.numpy as jnp
from jax import lax
import numpy as np

D_MODEL = 1024
BATCH = 16
SEQ = 256
DEPTH = 2
DEC_BATCH = 4
DEC_SEQ = 2048
PAST_LEN = 512

GRID_W = 64
HEAD_DIM = 64
N_BRANCH = 4
BRANCH_DIM = D_MODEL // N_BRANCH
NA_HEADS = BRANCH_DIM // HEAD_DIM
NA_WIN_H = 8
NA_WIN_W = 16
NA_QCOLS = 16
NA_KCOLS = 32
SWA_Q_HEADS = BRANCH_DIM // HEAD_DIM
SWA_KV_HEADS = 2
SWA_WINDOW = 128
SWA_BLOCK = 128
FNET_GROUPS = 4
FNET_GROUP_DIM = BRANCH_DIM // FNET_GROUPS
DIFF_HEADS = 4
DIFF_V_DIM = BRANCH_DIM // DIFF_HEADS
DIFF_QK_DIM = DIFF_V_DIM // 2
D_FF = ((-(-8 * D_MODEL // 3)) + 255) // 256 * 256
Q_BLOCK = 128
ROPE_BASE = 10000.0
NORM_EPS = 1e-6
NEG_INF = -1e30
NA_W = NA_HEADS * HEAD_DIM
SWA_QW = SWA_Q_HEADS * HEAD_DIM
SWA_KW = SWA_KV_HEADS * HEAD_DIM
FNET_W = FNET_GROUPS * FNET_GROUP_DIM
DIFF_QKW = DIFF_HEADS * 2 * DIFF_QK_DIM
DIFF_VW = DIFF_HEADS * DIFF_V_DIM
SPLIT_SIZES = (NA_W, NA_W, NA_W, SWA_QW, SWA_KW, SWA_KW, FNET_W, DIFF_QKW, DIFF_QKW, DIFF_VW)
D_IN = 3 * NA_W + SWA_QW + 2 * SWA_KW + FNET_W + 2 * DIFF_QKW + DIFF_VW

kernel_name = "hybrid_flow_prefix_step"


def rmsnorm(x, g):
    xf = x.astype(jnp.float32)
    y = xf * lax.rsqrt(jnp.mean(jnp.square(xf), axis=-1, keepdims=True) + NORM_EPS)
    return (y * g.astype(jnp.float32)).astype(x.dtype)


def rope_1d(x, pos):
    m = x.shape[-1]
    half = m // 2
    inv = ROPE_BASE ** (-jnp.arange(half, dtype=jnp.float32) * 2.0 / m)
    ang = pos.astype(jnp.float32)[:, None] * inv[None, :]
    cos = jnp.cos(ang)[:, None, :]
    sin = jnp.sin(ang)[:, None, :]
    xf = x.astype(jnp.float32)
    x1, x2 = xf[..., :half], xf[..., half:]
    return jnp.concatenate([x1 * cos - x2 * sin, x2 * cos + x1 * sin], axis=-1).astype(x.dtype)


def axial_rope(x, rows, cols):
    h = x.shape[-1] // 2
    return jnp.concatenate([rope_1d(x[..., :h], rows), rope_1d(x[..., h:], cols)], axis=-1)


def rope_diff(x, rows, cols):
    B, L, H, C, dq = x.shape
    return axial_rope(x.reshape(B, L, H * C, dq), rows, cols).reshape(x.shape)


def map_qblocks(fn, q):
    B, L = q.shape[:2]
    nb = L // Q_BLOCK
    qb = jnp.moveaxis(q.reshape((B, nb, Q_BLOCK) + q.shape[2:]), 1, 0)
    ob = lax.map(fn, qb)
    return jnp.moveaxis(ob, 0, 1).reshape((B, L) + ob.shape[3:])


def dense_attn(q, k, v, sink=None):
    B, L, Hq, d = q.shape
    Hkv = k.shape[2]
    G = Hq // Hkv
    scale = d ** -0.5

    def blk(qb):
        qg = qb.reshape(B, Q_BLOCK, Hkv, G, d)
        s = jnp.einsum('bqhgd,bkhd->bhgqk', qg, k).astype(jnp.float32) * scale
        if sink is not None:
            sk = jnp.broadcast_to(sink.astype(jnp.float32).reshape(1, Hkv, G, 1, 1), s.shape[:-1] + (1,))
            p = jax.nn.softmax(jnp.concatenate([sk, s], axis=-1), axis=-1)[..., 1:]
        else:
            p = jax.nn.softmax(s, axis=-1)
        o = jnp.einsum('bhgqk,bkhd->bqhgd', p.astype(v.dtype), v)
        return o.reshape(B, Q_BLOCK, Hq, d)

    return map_qblocks(blk, q)


def na_latent(q, k, v, ck, cv, rpb):
    B, L, H, d = q.shape
    R = L // GRID_W
    kh = min(NA_WIN_H, R)
    nb = GRID_W // NA_QCOLS
    kk = kh * NA_KCOLS
    scale = d ** -0.5
    r = np.arange(R)
    rs = np.clip(r - kh // 2, 0, R - kh)
    key_rows = rs[:, None] + np.arange(kh)
    kb = np.clip(np.arange(nb) * NA_QCOLS - NA_WIN_W // 2, 0, GRID_W - NA_KCOLS)
    key_cols = kb[:, None] + np.arange(NA_KCOLS)
    qcols = np.arange(GRID_W).reshape(nb, NA_QCOLS)
    cs = np.clip(qcols - NA_WIN_W // 2, 0, GRID_W - NA_WIN_W)
    idx = (key_rows[:, None, :, None] * GRID_W + key_cols[None, :, None, :]).reshape(R, nb, kk)
    kg = k[:, idx]
    vg = v[:, idx]
    qb = q.reshape(B, R, nb, NA_QCOLS, H, d)
    kc = key_cols[:, None, None, :]
    valid = (kc >= cs[:, :, None, None]) & (kc < cs[:, :, None, None] + NA_WIN_W)
    valid = np.broadcast_to(valid, (nb, NA_QCOLS, kh, NA_KCOLS)).reshape(nb, NA_QCOLS, kk)
    dr = key_rows - r[:, None] + NA_WIN_H - 1
    dc = np.clip(key_cols[:, None, :] - qcols[:, :, None] + NA_WIN_W - 1, 0, 2 * NA_WIN_W - 2)
    bias = rpb[:, dr[:, None, None, :, None], dc[None, :, :, None, :]]
    bias = jnp.transpose(bias, (1, 2, 0, 3, 4, 5)).reshape(R, nb, H, NA_QCOLS, kk).astype(jnp.float32)
    s_loc = jnp.einsum('brnqhd,brnkhd->brnhqk', qb, kg).astype(jnp.float32) * scale + bias
    s_loc = jnp.where(valid[None, None, :, None], s_loc, NEG_INF)
    s_ctx = jnp.einsum('brnqhd,bkhd->brnhqk', qb, ck).astype(jnp.float32) * scale
    p = jax.nn.softmax(jnp.concatenate([s_loc, s_ctx], axis=-1), axis=-1).astype(v.dtype)
    o = (jnp.einsum('brnhqk,brnkhd->brnqhd', p[..., :kk], vg)
         + jnp.einsum('brnhqk,bkhd->brnqhd', p[..., kk:], cv))
    return o.reshape(B, L, H, d)


def swa_latent(q, k, v, ck, cv, sink):
    B, L, Hq, d = q.shape
    Hkv = k.shape[2]
    G = Hq // Hkv
    nb = L // SWA_BLOCK
    kw = 3 * SWA_BLOCK
    scale = d ** -0.5
    pad = ((0, 0), (SWA_BLOCK, SWA_BLOCK), (0, 0), (0, 0))
    kp = jnp.pad(k, pad)
    vp = jnp.pad(v, pad)
    idx = (np.arange(nb) * SWA_BLOCK)[:, None] + np.arange(kw)
    kbl = kp[:, idx]
    vbl = vp[:, idx]
    kpos = idx - SWA_BLOCK
    qpos = (np.arange(nb) * SWA_BLOCK)[:, None] + np.arange(SWA_BLOCK)
    valid = ((np.abs(kpos[:, None, :] - qpos[:, :, None]) <= SWA_WINDOW)
             & (kpos >= 0)[:, None, :] & (kpos < L)[:, None, :])
    qb = q.reshape(B, nb, SWA_BLOCK, Hkv, G, d)
    s_loc = jnp.einsum('bnqhgd,bnkhd->bnhgqk', qb, kbl).astype(jnp.float32) * scale
    s_loc = jnp.where(valid[None, :, None, None], s_loc, NEG_INF)
    s_ctx = jnp.einsum('bnqhgd,bkhd->bnhgqk', qb, ck).astype(jnp.float32) * scale
    s_sink = jnp.broadcast_to(sink.astype(jnp.float32).reshape(1, 1, Hkv, G, 1, 1), s_loc.shape[:-1] + (1,))
    p = jax.nn.softmax(jnp.concatenate([s_sink, s_loc, s_ctx], axis=-1), axis=-1).astype(v.dtype)
    o = (jnp.einsum('bnhgqk,bnkhd->bnqhgd', p[..., 1:1 + kw], vbl)
         + jnp.einsum('bnhgqk,bkhd->bnqhgd', p[..., 1 + kw:], cv))
    return o.reshape(B, L, Hq, d)


def fourier_mix(u):
    y = jnp.fft.fft2(u.astype(jnp.float32), axes=(1, 3), norm='ortho').real
    return y.astype(u.dtype)


def lambda_value(lp, lam_init):
    lp = lp.astype(jnp.float32)
    return jnp.exp(jnp.sum(lp[0] * lp[1])) - jnp.exp(jnp.sum(lp[2] * lp[3])) + lam_init


def diff_attn(q, k, v, lam, lam_init, subln_g):
    scale = DIFF_QK_DIM ** -0.5

    def blk(qb):
        s = jnp.einsum('bqhcd,bkhcd->bhcqk', qb, k).astype(jnp.float32) * scale
        p = jax.nn.softmax(s, axis=-1)
        a = p[:, :, 0] - lam * p[:, :, 1]
        return jnp.einsum('bhqk,bkhd->bqhd', a.astype(v.dtype), v)

    o = map_qblocks(blk, q)
    return rmsnorm(o, subln_g) * (1.0 - lam_init)


def adaln(cond, ada_w, ada_b):
    m = jax.nn.silu(cond) @ ada_w + ada_b
    return tuple(t[:, None, :] for t in jnp.split(m, 6, axis=-1))


def project(h, w_in):
    B, L, _ = h.shape
    p = h @ w_in
    parts = []
    start = 0
    for w in SPLIT_SIZES:
        parts.append(p[..., start:start + w])
        start += w
    na_q, na_k, na_v = [t.reshape(B, L, NA_HEADS, HEAD_DIM) for t in parts[0:3]]
    swa_q = parts[3].reshape(B, L, SWA_Q_HEADS, HEAD_DIM)
    swa_k = parts[4].reshape(B, L, SWA_KV_HEADS, HEAD_DIM)
    swa_v = parts[5].reshape(B, L, SWA_KV_HEADS, HEAD_DIM)
    fu = parts[6].reshape(B, L, FNET_GROUPS, FNET_GROUP_DIM)
    dq = parts[7].reshape(B, L, DIFF_HEADS, 2, DIFF_QK_DIM)
    dk = parts[8].reshape(B, L, DIFF_HEADS, 2, DIFF_QK_DIM)
    dv = parts[9].reshape(B, L, DIFF_HEADS, DIFF_V_DIM)
    return na_q, na_k, na_v, swa_q, swa_k, swa_v, fu, dq, dk, dv


def finish_layer(x, h, branches, g1, sh2, sc2, g2, w_branch, w_gate, b_gate, w_o, norm2_g, w_ffn_in, w_ffn_out):
    B, L, _ = x.shape
    br = jnp.stack([b.reshape(B, L, BRANCH_DIM) for b in branches], axis=2)
    up = jnp.einsum('blkc,kcd->blkd', br, w_branch)
    gates = jax.nn.sigmoid(h @ w_gate + b_gate).reshape(B, L, N_BRANCH, D_MODEL)
    x = x + g1 * (jnp.sum(gates * up, axis=2) @ w_o)
    h2 = rmsnorm(x, norm2_g) * (1.0 + sc2) + sh2
    a, b = jnp.split(h2 @ w_ffn_in, 2, axis=-1)
    return x + g2 * ((jax.nn.silu(a) * b) @ w_ffn_out)


def setup_inputs(seed: int = 0) -> dict:
    key = jax.random.key(seed)
    ks = jax.random.split(key, 24)
    f32 = jnp.float32

    def nrm(k, shape, s):
        return jax.random.normal(k, shape, f32) * s

    return {
        "x_prompt": nrm(ks[0], (BATCH, SEQ, D_MODEL), 1.0),
        "x_sample": nrm(ks[1], (DEC_BATCH, DEC_SEQ, D_MODEL), 1.0),
        "cache_na_kv": nrm(ks[2], (DEC_BATCH, DEPTH, 2, PAST_LEN, NA_HEADS, HEAD_DIM), 1.0),
        "cache_swa_kv": nrm(ks[3], (DEC_BATCH, DEPTH, 2, PAST_LEN, SWA_KV_HEADS, HEAD_DIM), 1.0),
        "cache_diff_kv": nrm(ks[4], (DEC_BATCH, DEPTH, 2, PAST_LEN, DIFF_HEADS, DIFF_V_DIM), 1.0),
        "c": nrm(ks[5], (DEC_BATCH, D_MODEL), 1.0),
        "c_ctx": nrm(ks[6], (D_MODEL,), 1.0),
        "norm1_g": 1.0 + nrm(ks[7], (DEPTH, D_MODEL), 0.02),
        "norm2_g": 1.0 + nrm(ks[8], (DEPTH, D_MODEL), 0.02),
        "ada_w": nrm(ks[9], (DEPTH, D_MODEL, 6 * D_MODEL), 0.5 * D_MODEL ** -0.5),
        "ada_b": nrm(ks[10], (DEPTH, 6 * D_MODEL), 0.02),
        "w_in": nrm(ks[11], (DEPTH, D_MODEL, D_IN), D_MODEL ** -0.5),
        "na_rpb": nrm(ks[12], (DEPTH, NA_HEADS, 2 * NA_WIN_H - 1, 2 * NA_WIN_W - 1), 0.02),
        "swa_sink": nrm(ks[13], (DEPTH, SWA_Q_HEADS), 1.0),
        "diff_lambda": nrm(ks[14], (DEPTH, 4, DIFF_QK_DIM), 0.1),
        "diff_subln_g": 1.0 + nrm(ks[15], (DEPTH, DIFF_V_DIM), 0.02),
        "w_branch": nrm(ks[16], (DEPTH, N_BRANCH, BRANCH_DIM, D_MODEL), BRANCH_DIM ** -0.5),
        "w_gate": nrm(ks[17], (DEPTH, D_MODEL, N_BRANCH * D_MODEL), D_MODEL ** -0.5),
        "b_gate": nrm(ks[18], (DEPTH, N_BRANCH * D_MODEL), 0.02),
        "w_o": nrm(ks[19], (DEPTH, D_MODEL, D_MODEL), D_MODEL ** -0.5),
        "w_ffn_in": nrm(ks[20], (DEPTH, D_MODEL, 2 * D_FF), D_MODEL ** -0.5),
        "w_ffn_out": nrm(ks[21], (DEPTH, D_FF, D_MODEL), D_FF ** -0.5),
        "final_norm_g": 1.0 + nrm(ks[22], (D_MODEL,), 0.02),
    }


def reference(x_prompt, x_sample, cache_na_kv, cache_swa_kv, cache_diff_kv, c, c_ctx,
              norm1_g, norm2_g, ada_w, ada_b, w_in, na_rpb, swa_sink, diff_lambda, diff_subln_g,
              w_branch, w_gate, b_gate, w_o, w_ffn_in, w_ffn_out, final_norm_g):
    xp = x_prompt
    B, Lc, _ = xp.shape
    na_states, swa_states, diff_states = [], [], []
    for l in range(DEPTH):
        lam_init = 0.8 - 0.6 * math.exp(-0.3 * l)
        lam = lambda_value(diff_lambda[l], lam_init)
        sh1, sc1, g1, sh2, sc2, g2 = adaln(c_ctx[None, :], ada_w[l], ada_b[l])
        h = rmsnorm(xp, norm1_g[l]) * (1.0 + sc1) + sh1
        na_q, na_k, na_v, swa_q, swa_k, swa_v, fu, dq, dk, dv = project(h, w_in[l])
        o_na = dense_attn(na_q, na_k, na_v)
        o_swa = dense_attn(swa_q, swa_k, swa_v, swa_sink[l])
        o_f = fourier_mix(fu)
        o_d = diff_attn(dq, dk, dv, lam, lam_init, diff_subln_g[l])
        xp = finish_layer(xp, h, [o_na, o_swa, o_f, o_d], g1, sh2, sc2, g2, w_branch[l], w_gate[l],
                          b_gate[l], w_o[l], norm2_g[l], w_ffn_in[l], w_ffn_out[l])
        na_states.append(jnp.stack([na_k, na_v], axis=1))
        swa_states.append(jnp.stack([swa_k, swa_v], axis=1))
        diff_states.append(jnp.stack([dk.reshape(B, Lc, DIFF_HEADS, 2 * DIFF_QK_DIM), dv], axis=1))
    y_prompt = rmsnorm(xp, final_norm_g)
    new_na_kv = jnp.stack(na_states, axis=1)
    new_swa_kv = jnp.stack(swa_states, axis=1)
    new_diff_kv = jnp.stack(diff_states, axis=1)

    xs = x_sample
    Bd, L, _ = xs.shape
    Lp = cache_diff_kv.shape[3]
    pos = jnp.arange(L)
    rows = pos // GRID_W
    cols = pos % GRID_W
    for l in range(DEPTH):
        lam_init = 0.8 - 0.6 * math.exp(-0.3 * l)
        lam = lambda_value(diff_lambda[l], lam_init)
        sh1, sc1, g1, sh2, sc2, g2 = adaln(c, ada_w[l], ada_b[l])
        h = rmsnorm(xs, norm1_g[l]) * (1.0 + sc1) + sh1
        na_q, na_k, na_v, swa_q, swa_k, swa_v, fu, dq, dk, dv = project(h, w_in[l])
        o_na = na_latent(na_q, na_k, na_v, cache_na_kv[:, l, 0], cache_na_kv[:, l, 1], na_rpb[l])
        o_swa = swa_latent(axial_rope(swa_q, rows, cols), axial_rope(swa_k, rows, cols), swa_v,
                           cache_swa_kv[:, l, 0], cache_swa_kv[:, l, 1], swa_sink[l])
        o_f = fourier_mix(fu)
        ck = cache_diff_kv[:, l, 0].reshape(Bd, Lp, DIFF_HEADS, 2, DIFF_QK_DIM)
        k_all = jnp.concatenate([rope_diff(dk, rows, cols), ck], axis=1)
        v_all = jnp.concatenate([dv, cache_diff_kv[:, l, 1]], axis=1)
        o_d = diff_attn(rope_diff(dq, rows, cols), k_all, v_all, lam, lam_init, diff_subln_g[l])
        xs = finish_layer(xs, h, [o_na, o_swa, o_f, o_d], g1, sh2, sc2, g2, w_branch[l], w_gate[l],
                          b_gate[l], w_o[l], norm2_g[l], w_ffn_in[l], w_ffn_out[l])
    y_sample = rmsnorm(xs, final_norm_g)
    return (y_prompt, y_sample, new_na_kv, new_swa_kv, new_diff_kv)
```

```python
import functools
import math

import numpy as np
import jax
import jax.numpy as jnp
from jax import lax
from jax.experimental import pallas as pl
from jax.experimental.pallas import tpu as pltpu

D_MODEL = 1024
BATCH = 16
SEQ = 256
DEPTH = 2
DEC_BATCH = 4
DEC_SEQ = 2048
PAST_LEN = 512
GRID_W = 64
GRID_H = DEC_SEQ // GRID_W
HEAD_DIM = 64
NA_WIN_H = 8
NA_WIN_W = 16
SWA_WINDOW = 128
DIFF_QK_DIM = 32
D_FF = 2816
D_IN = 2304
ROPE_BASE = 10000.0
NORM_EPS = 1e-6
NEG_INF = -1e30

NA_Q, NA_K, NA_V = 0, 256, 512
SWA_Q, SWA_K, SWA_V = 768, 1024, 1152
FNET_U = 1280
DIFF_Q, DIFF_K, DIFF_V = 1536, 1792, 2048

LANES = 128
N_CTX_TOK = BATCH * SEQ
N_LAT_TOK = DEC_BATCH * DEC_SEQ
N_TOK = N_CTX_TOK + N_LAT_TOK
COND_ROWS = 8
TOKEN_TILE = 256
Q_TILE = 256
NA_TILE_ROWS = Q_TILE // GRID_W
NA_BAND_ROWS = NA_WIN_H + NA_TILE_ROWS
SWA_KEYS = 2 * Q_TILE
VMEM_LIMIT = 56 * 1024 * 1024

F32 = jnp.float32
BF16 = jnp.bfloat16


def _params(semantics, vmem=None):
    return pltpu.CompilerParams(dimension_semantics=semantics, vmem_limit_bytes=vmem)


def _dot(a, b):
    return jnp.dot(a, b, preferred_element_type=F32)


def _dot_nt(a, b):
    return lax.dot_general(a, b, (((1,), (1,)), ((), ())), preferred_element_type=F32)


def _split(x):
    hi = x.astype(BF16)
    lo = (x - hi.astype(F32)).astype(BF16)
    return hi, lo


def _dot3(a_hi, a_lo, b_hi, b_lo):
    return _dot(a_hi, b_hi) + _dot(a_lo, b_hi) + _dot(a_hi, b_lo)


def _sigmoid(x):
    return 1.0 / (1.0 + jnp.exp(-x))


def _rms(x, g):
    return x * lax.rsqrt(jnp.mean(x * x, axis=-1, keepdims=True) + NORM_EPS) * g


def _exp_parts(blocks, extra=None):
    m = None
    for s in blocks:
        mi = jnp.max(s, axis=-1, keepdims=True)
        m = mi if m is None else jnp.maximum(m, mi)
    if extra is not None:
        m = jnp.maximum(m, extra)
    es = [jnp.exp(s - m) for s in blocks]
    l = None
    for e in es:
        li = jnp.sum(e, axis=-1, keepdims=True)
        l = li if l is None else l + li
    if extra is not None:
        l = l + jnp.exp(extra - m)
    return es, l


def _lane(shape):
    return lax.broadcasted_iota(jnp.int32, shape, 1)


def _rope(x, cos, sin_signed, half):
    lane = _lane(x.shape)
    partner = jnp.where((lane % (2 * half)) < half,
                        pltpu.roll(x, LANES - half, 1), pltpu.roll(x, half, 1))
    return x * cos + partner * sin_signed


def _lambda(lp, lam_init):
    s1 = jnp.sum(lp[0:1, :] * lp[1:2, :], axis=-1, keepdims=True)
    s2 = jnp.sum(lp[2:3, :] * lp[3:4, :], axis=-1, keepdims=True)
    return jnp.exp(s1) - jnp.exp(s2) + lam_init


def _subln(o, g2, lam_init):
    lane = _lane(o.shape)
    sq = o * o
    ms0 = jnp.sum(jnp.where(lane < HEAD_DIM, sq, 0.0), axis=-1, keepdims=True)
    ms1 = jnp.sum(jnp.where(lane >= HEAD_DIM, sq, 0.0), axis=-1, keepdims=True)
    ms = jnp.where(lane < HEAD_DIM, ms0, ms1) * (1.0 / HEAD_DIM)
    return (o * lax.rsqrt(ms + NORM_EPS) * g2) * (1.0 - lam_init)


def _adaln_kernel(cond_ref, w_ref, b_ref, o_ref):
    c = cond_ref[...]
    s = c * _sigmoid(c)
    s_hi, s_lo = _split(s)
    w_hi, w_lo = _split(w_ref[...])
    o_ref[...] = _dot3(s_hi, s_lo, w_hi, w_lo) + b_ref[...]


def _adaln(cond, ada_w, ada_b):
    tn = 1536
    n = 6 * D_MODEL
    return pl.pallas_call(
        _adaln_kernel,
        grid=(DEPTH, n // tn),
        in_specs=[
            pl.BlockSpec((COND_ROWS, D_MODEL), lambda l, j: (0, 0)),
            pl.BlockSpec((None, D_MODEL, tn), lambda l, j: (l, 0, j)),
            pl.BlockSpec((None, 1, tn), lambda l, j: (l, 0, j)),
        ],
        out_specs=pl.BlockSpec((None, COND_ROWS, tn), lambda l, j: (l, 0, j)),
        out_shape=jax.ShapeDtypeStruct((DEPTH, COND_ROWS, n), F32),
        compiler_params=_params(("arbitrary", "arbitrary")),
        name="adaln",
    )(cond, ada_w, ada_b.reshape(DEPTH, 1, n))


def _cond_row(i):
    n_ctx = N_CTX_TOK // TOKEN_TILE
    per_seq = DEC_SEQ // TOKEN_TILE
    return jnp.where(i < n_ctx, 0, 1 + (i - n_ctx) // per_seq)


def _mod_spec(layer):
    return pl.BlockSpec((None, None, 6, D_MODEL), lambda i: (layer, _cond_row(i), 0, 0))


def _tok_spec(width):
    return pl.BlockSpec((TOKEN_TILE, width), lambda i: (i, 0))


def _const_spec(shape):
    nd = len(shape)
    return pl.BlockSpec(shape, lambda i: (0,) * nd)


def _proj_kernel(x_ref, mod_ref, g_ref, w_ref, p_ref):
    h = _rms(x_ref[...], g_ref[...]) * (1.0 + mod_ref[1:2, :]) + mod_ref[0:1, :]
    p_ref[...] = _dot(h.astype(BF16), w_ref[...])


def _proj(x, mod, layer, g, w):
    return pl.pallas_call(
        _proj_kernel,
        grid=(N_TOK // TOKEN_TILE,),
        in_specs=[_tok_spec(D_MODEL), _mod_spec(layer), _const_spec((1, D_MODEL)),
                  _const_spec((D_MODEL, D_IN))],
        out_specs=_tok_spec(D_IN),
        out_shape=jax.ShapeDtypeStruct((N_TOK, D_IN), F32),
        compiler_params=_params(("arbitrary",), VMEM_LIMIT),
        name="proj",
    )(x, mod, g, w)


def _merge_kernel(x_ref, mod_ref, g_ref, b0_ref, b1_ref, b2_ref, b3_ref, wg_ref, bg_ref, wb_ref, wo_ref,
                  o_ref):
    x = x_ref[...]
    h = _rms(x, g_ref[...]) * (1.0 + mod_ref[1:2, :]) + mod_ref[0:1, :]
    hb = h.astype(BF16)
    merged = None
    for k, b_ref in enumerate((b0_ref, b1_ref, b2_ref, b3_ref)):
        cols = slice(k * D_MODEL, (k + 1) * D_MODEL)
        gate = _sigmoid(_dot(hb, wg_ref[:, cols]) + bg_ref[:, cols])
        term = gate * _dot(b_ref[...], wb_ref[k])
        merged = term if merged is None else merged + term
    o_ref[...] = x + mod_ref[2:3, :] * _dot(merged.astype(BF16), wo_ref[...])


def _merge(x, mod, layer, g, branches, wg, bg, wb, wo):
    nb = 4
    bw = D_MODEL // nb
    return pl.pallas_call(
        _merge_kernel,
        grid=(N_TOK // TOKEN_TILE,),
        in_specs=[_tok_spec(D_MODEL), _mod_spec(layer), _const_spec((1, D_MODEL))]
                 + [_tok_spec(bw)] * nb
                 + [_const_spec((D_MODEL, nb * D_MODEL)), _const_spec((1, nb * D_MODEL)),
                    _const_spec((nb, bw, D_MODEL)), _const_spec((D_MODEL, D_MODEL))],
        out_specs=_tok_spec(D_MODEL),
        out_shape=jax.ShapeDtypeStruct((N_TOK, D_MODEL), F32),
        compiler_params=_params(("arbitrary",), VMEM_LIMIT),
        name="merge",
    )(x, mod, g, *branches, wg, bg, wb, wo)


def _ffn_kernel(x_ref, mod_ref, g_ref, wi_ref, wo_ref, fg_ref, o_ref, *, final):
    x = x_ref[...]
    h = _rms(x, g_ref[...]) * (1.0 + mod_ref[4:5, :]) + mod_ref[3:4, :]
    hb = h.astype(BF16)
    a = _dot(hb, wi_ref[:, :D_FF])
    b = _dot(hb, wi_ref[:, D_FF:])
    f = (a * _sigmoid(a)) * b
    y = x + mod_ref[5:6, :] * _dot(f.astype(BF16), wo_ref[...])
    o_ref[...] = _rms(y, fg_ref[...]) if final else y


def _ffn(x, mod, layer, g, wi, wo, fg, final):
    return pl.pallas_call(
        functools.partial(_ffn_kernel, final=final),
        grid=(N_TOK // TOKEN_TILE,),
        in_specs=[_tok_spec(D_MODEL), _mod_spec(layer), _const_spec((1, D_MODEL)),
                  _const_spec((D_MODEL, 2 * D_FF)), _const_spec((D_FF, D_MODEL)),
                  _const_spec((1, D_MODEL))],
        out_specs=_tok_spec(D_MODEL),
        out_shape=jax.ShapeDtypeStruct((N_TOK, D_MODEL), F32),
        compiler_params=_params(("arbitrary",), VMEM_LIMIT),
        name="ffn",
    )(x, mod, g, wi, wo, fg)


@functools.lru_cache(maxsize=None)
def _dft_tables(n, blocks):
    k = np.arange(n, dtype=np.int64)
    ang = 2.0 * np.pi * ((k[:, None] * k[None, :]) % n).astype(np.float64) / n
    out = []
    for m in (np.cos(ang), np.sin(ang)):
        m = np.kron(np.eye(blocks), m / math.sqrt(n)).astype(np.float32)
        hi = m.astype(BF16)
        lo = (m - hi.astype(np.float32)).astype(BF16)
        out += [hi, lo]
    return tuple(out)


def _fourier_kernel(u_ref, ch_ref, cl_ref, sh_ref, sl_ref, cch_ref, ccl_ref, sch_ref, scl_ref, o_ref):
    u_hi, u_lo = _split(u_ref[...])
    a_hi, a_lo = _split(_dot3(ch_ref[...], cl_ref[...], u_hi, u_lo))
    b_hi, b_lo = _split(_dot3(sh_ref[...], sl_ref[...], u_hi, u_lo))
    y = _dot3(a_hi, a_lo, cch_ref[...], ccl_ref[...]) - _dot3(b_hi, b_lo, sch_ref[...], scl_ref[...])
    o_ref[...] = y.astype(o_ref.dtype)


def _fourier(p, seq, n_seq, first_tok):
    tf = 256
    nf = seq // tf
    width = 256
    first_blk = first_tok // seq
    pos = [jnp.asarray(t) for t in _dft_tables(seq, 1)]
    chan = [jnp.asarray(t) for t in _dft_tables(width // 4, 4)]
    pos_spec = pl.BlockSpec((tf, seq), lambda f, b: (f, 0))
    chan_spec = pl.BlockSpec((width, width), lambda f, b: (0, 0))
    return pl.pallas_call(
        _fourier_kernel,
        grid=(nf, n_seq),
        in_specs=[pl.BlockSpec((seq, width), lambda f, b: (first_blk + b, FNET_U // width))]
                 + [pos_spec] * 4 + [chan_spec] * 4,
        out_specs=pl.BlockSpec((tf, width), lambda f, b: (b * nf + f, 0)),
        out_shape=jax.ShapeDtypeStruct((n_seq * seq, width), BF16),
        compiler_params=_params(("arbitrary", "arbitrary"), VMEM_LIMIT),
        name="fourier",
    )(p, *pos, *chan)


def _ctx_mixer_kernel(p_ref, sink_ref, lp_ref, sg_ref, na_ref, swa_ref, diff_ref, *, lam_init):
    shape = (SEQ, LANES)
    lane = _lane(shape)
    half = lane // HEAD_DIM
    quarter = lane // DIFF_QK_DIM
    scale = HEAD_DIM ** -0.5

    def pair(col, j):
        return p_ref[:, col + LANES * j: col + LANES * (j + 1)]

    for j in range(2):
        q2 = pair(NA_Q, j) * scale
        kb = pair(NA_K, j).astype(BF16)
        vb = pair(NA_V, j).astype(BF16)
        outs = []
        for g in range(2):
            qm = jnp.where(half == g, q2, 0.0).astype(BF16)
            (e,), l = _exp_parts([_dot_nt(qm, kb)])
            outs.append(_dot(e.astype(BF16), vb) / l)
        na_ref[:, LANES * j: LANES * (j + 1)] = jnp.where(half == 0, outs[0], outs[1]).astype(na_ref.dtype)

    k2 = p_ref[:, SWA_K: SWA_K + LANES]
    v2 = p_ref[:, SWA_V: SWA_V + LANES]
    k2s = pltpu.roll(k2, HEAD_DIM, 1)
    v2s = pltpu.roll(v2, HEAD_DIM, 1)
    for j in range(2):
        kb = jnp.where(half == j, k2, k2s).astype(BF16)
        vb = jnp.where(half == j, v2, v2s).astype(BF16)
        q2 = pair(SWA_Q, j) * scale
        outs = []
        for g in range(2):
            qm = jnp.where(half == g, q2, 0.0).astype(BF16)
            (e,), l = _exp_parts([_dot_nt(qm, kb)], extra=sink_ref[2 * j + g])
            outs.append(_dot(e.astype(BF16), vb) / l)
        swa_ref[:, LANES * j: LANES * (j + 1)] = jnp.where(half == 0, outs[0], outs[1]).astype(swa_ref.dtype)

    lam = _lambda(lp_ref[...], lam_init)
    dscale = DIFF_QK_DIM ** -0.5
    for j in range(2):
        q2 = pair(DIFF_Q, j)
        kb = pair(DIFF_K, j).astype(BF16)
        vb = pair(DIFF_V, j).astype(BF16)
        outs = []
        for g in range(2):
            o = []
            for c in range(2):
                qm = jnp.where(quarter == 2 * g + c, q2, 0.0).astype(BF16)
                (e,), l = _exp_parts([_dot_nt(qm, kb) * dscale])
                o.append(_dot(e.astype(BF16), vb) / l)
            outs.append(o[0] - lam * o[1])
        o2 = jnp.where(half == 0, outs[0], outs[1])
        diff_ref[:, LANES * j: LANES * (j + 1)] = _subln(o2, sg_ref[...], lam_init).astype(diff_ref.dtype)


def _ctx_mixers(p, sink, lp, sg2, lam_init):
    out = jax.ShapeDtypeStruct((N_CTX_TOK, 256), BF16)
    ospec = pl.BlockSpec((SEQ, 256), lambda b: (b, 0))
    return pl.pallas_call(
        functools.partial(_ctx_mixer_kernel, lam_init=lam_init),
        grid=(BATCH,),
        in_specs=[pl.BlockSpec((SEQ, D_IN), lambda b: (b, 0)),
                  pl.BlockSpec(memory_space=pltpu.SMEM),
                  pl.BlockSpec((4, DIFF_QK_DIM), lambda b: (0, 0)),
                  pl.BlockSpec((1, LANES), lambda b: (0, 0))],
        out_specs=[ospec, ospec, ospec],
        out_shape=[out, out, out],
        compiler_params=_params(("arbitrary",), VMEM_LIMIT),
        name="ctx_mixers",
    )(p, sink, lp, sg2)


def _na_band_start(t):
    return jnp.clip(NA_TILE_ROWS * t - NA_WIN_H // 2, 0, GRID_H - NA_BAND_ROWS)


@functools.lru_cache(maxsize=None)
def _na_bias_index():
    pats = []
    for t in (0, 1, GRID_H // NA_TILE_ROWS - 1):
        bs = int(np.clip(NA_TILE_ROWS * t - NA_WIN_H // 2, 0, GRID_H - NA_BAND_ROWS))
        r = NA_TILE_ROWS * t + np.arange(NA_TILE_ROWS)
        c = np.arange(GRID_W)
        kr = bs + np.arange(NA_BAND_ROWS)
        kc = np.arange(GRID_W)
        rs = np.clip(r - NA_WIN_H // 2, 0, GRID_H - NA_WIN_H)
        cs = np.clip(c - NA_WIN_W // 2, 0, GRID_W - NA_WIN_W)
        vr = (kr[None, :] >= rs[:, None]) & (kr[None, :] < rs[:, None] + NA_WIN_H)
        vc = (kc[None, :] >= cs[:, None]) & (kc[None, :] < cs[:, None] + NA_WIN_W)
        dr = np.clip(kr[None, :] - r[:, None] + NA_WIN_H - 1, 0, 2 * NA_WIN_H - 2)
        dc = np.clip(kc[None, :] - c[:, None] + NA_WIN_W - 1, 0, 2 * NA_WIN_W - 2)
        nq, nk = NA_TILE_ROWS * GRID_W, NA_BAND_ROWS * GRID_W
        valid = (vr[:, None, :, None] & vc[None, :, None, :]).reshape(nq, nk)
        dr = np.broadcast_to(dr[:, None, :, None], (NA_TILE_ROWS, GRID_W, NA_BAND_ROWS, GRID_W)).reshape(nq, nk)
        dc = np.broadcast_to(dc[None, :, None, :], (NA_TILE_ROWS, GRID_W, NA_BAND_ROWS, GRID_W)).reshape(nq, nk)
        pats.append((dr, dc, valid))
    return tuple(np.stack(a) for a in zip(*pats))


def _na_bias_table(rpb):
    dr, dc, valid = _na_bias_index()
    tab = jnp.where(valid[None], rpb[:, dr, dc], NEG_INF)
    return jnp.transpose(tab, (1, 0, 2, 3))


def _lat_na_kernel(q_ref, k_ref, v_ref, ck_ref, cv_ref, bias_ref, o_ref):
    t = pl.program_id(2)
    start = pl.multiple_of(_na_band_start(t) * GRID_W, GRID_W)
    nk = NA_BAND_ROWS * GRID_W
    kb = k_ref[pl.ds(start, nk), :].astype(BF16)
    vb = v_ref[pl.ds(start, nk), :].astype(BF16)
    ckb = ck_ref[...].astype(BF16)
    cvb = cv_ref[...].astype(BF16)
    q2 = q_ref[...] * (HEAD_DIM ** -0.5)
    half = _lane(q2.shape) // HEAD_DIM
    outs = []
    for g in range(2):
        qm = jnp.where(half == g, q2, 0.0).astype(BF16)
        (e_loc, e_ctx), l = _exp_parts([_dot_nt(qm, kb) + bias_ref[g], _dot_nt(qm, ckb)])
        outs.append((_dot(e_loc.astype(BF16), vb) + _dot(e_ctx.astype(BF16), cvb)) / l)
    o_ref[...] = jnp.where(half == 0, outs[0], outs[1]).astype(o_ref.dtype)


def _lat_blocks(col):
    qt = DEC_SEQ // Q_TILE
    first_q = N_CTX_TOK // Q_TILE
    first_k = N_CTX_TOK // DEC_SEQ
    q = pl.BlockSpec((Q_TILE, LANES), lambda b, j, t: (first_q + qt * b + t, col[0] // LANES + j))
    k = pl.BlockSpec((DEC_SEQ, LANES), lambda b, j, t: (first_k + b, col[1] // LANES + j))
    v = pl.BlockSpec((DEC_SEQ, LANES), lambda b, j, t: (first_k + b, col[2] // LANES + j))
    o = pl.BlockSpec((Q_TILE, LANES), lambda b, j, t: (qt * b + t, j))
    return q, k, v, o


def _cache_spec(layer, which, shared_kv):
    return pl.BlockSpec((None, None, None, PAST_LEN, LANES),
                        lambda b, j, t: (b, layer, which, 0, 0 if shared_kv else j))


def _lat_na(p, cache, bias, layer):
    q, k, v, o = _lat_blocks((NA_Q, NA_K, NA_V))
    n_pat = GRID_H // NA_TILE_ROWS - 1

    def pattern(t):
        return jnp.where(t == 0, 0, jnp.where(t == n_pat, 2, 1))

    bias_spec = pl.BlockSpec((None, 2, Q_TILE, NA_BAND_ROWS * GRID_W), lambda b, j, t: (pattern(t), j, 0, 0))
    return pl.pallas_call(
        _lat_na_kernel,
        grid=(DEC_BATCH, 2, DEC_SEQ // Q_TILE),
        in_specs=[q, k, v, _cache_spec(layer, 0, False), _cache_spec(layer, 1, False), bias_spec],
        out_specs=o,
        out_shape=jax.ShapeDtypeStruct((N_LAT_TOK, 256), BF16),
        compiler_params=_params(("arbitrary",) * 3, VMEM_LIMIT),
        name="lat_na",
    )(p, p, p, cache, cache, bias)


@functools.lru_cache(maxsize=None)
def _rope_tables(dim):
    quarter = dim // 4
    pos = np.arange(DEC_SEQ)
    rows, cols = pos // GRID_W, pos % GRID_W
    lane = np.arange(LANES)
    w = lane % dim
    axis_pos = np.where((w // (dim // 2) == 0)[None, :], rows[:, None], cols[:, None]).astype(np.float64)
    u = w % (dim // 2)
    inv = ROPE_BASE ** (-(u % quarter).astype(np.float64) * 2.0 / (dim // 2))
    ang = axis_pos * inv[None, :]
    sign = np.where(u < quarter, -1.0, 1.0)[None, :]
    return np.cos(ang).astype(np.float32), (np.sin(ang) * sign).astype(np.float32)


def _swa_key_start(t):
    return jnp.clip(Q_TILE * t - SWA_WINDOW, 0, DEC_SEQ - SWA_KEYS)


def _lat_swa_kernel(q_ref, k_ref, v_ref, ck_ref, cv_ref, cos_ref, sin_ref, sink_ref, o_ref):
    j = pl.program_id(1)
    t = pl.program_id(2)
    q0 = pl.multiple_of(t * Q_TILE, Q_TILE)
    k0 = pl.multiple_of(_swa_key_start(t), SWA_WINDOW)
    quarter = HEAD_DIM // 4
    q2 = _rope(q_ref[...], cos_ref[pl.ds(q0, Q_TILE), :], sin_ref[pl.ds(q0, Q_TILE), :], quarter)
    q2 = q2 * (HEAD_DIM ** -0.5)
    k2 = _rope(k_ref[pl.ds(k0, SWA_KEYS), :], cos_ref[pl.ds(k0, SWA_KEYS), :], sin_ref[pl.ds(k0, SWA_KEYS), :],
               quarter)
    v2 = v_ref[pl.ds(k0, SWA_KEYS), :]
    ck2 = ck_ref[...]
    cv2 = cv_ref[...]

    def head_j(x):
        h = _lane(x.shape) // HEAD_DIM
        return jnp.where(h == j, x, pltpu.roll(x, HEAD_DIM, 1)).astype(BF16)

    kb, vb, ckb, cvb = head_j(k2), head_j(v2), head_j(ck2), head_j(cv2)
    qpos = q0 + lax.broadcasted_iota(jnp.int32, (Q_TILE, SWA_KEYS), 0)
    kpos = k0 + lax.broadcasted_iota(jnp.int32, (Q_TILE, SWA_KEYS), 1)
    valid = jnp.abs(kpos - qpos) <= SWA_WINDOW
    half = _lane(q2.shape) // HEAD_DIM
    outs = []
    for g in range(2):
        qm = jnp.where(half == g, q2, 0.0).astype(BF16)
        s_loc = jnp.where(valid, _dot_nt(qm, kb), NEG_INF)
        (e_loc, e_ctx), l = _exp_parts([s_loc, _dot_nt(qm, ckb)], extra=sink_ref[2 * j + g])
        outs.append((_dot(e_loc.astype(BF16), vb) + _dot(e_ctx.astype(BF16), cvb)) / l)
    o_ref[...] = jnp.where(half == 0, outs[0], outs[1]).astype(o_ref.dtype)


def _lat_swa(p, cache, sink, layer):
    q, _, _, o = _lat_blocks((SWA_Q, SWA_K, SWA_V))
    first_k = N_CTX_TOK // DEC_SEQ
    k = pl.BlockSpec((DEC_SEQ, LANES), lambda b, j, t: (first_k + b, SWA_K // LANES))
    v = pl.BlockSpec((DEC_SEQ, LANES), lambda b, j, t: (first_k + b, SWA_V // LANES))
    cos, sin = (jnp.asarray(a) for a in _rope_tables(HEAD_DIM))
    tab = pl.BlockSpec((DEC_SEQ, LANES), lambda b, j, t: (0, 0))
    return pl.pallas_call(
        _lat_swa_kernel,
        grid=(DEC_BATCH, 2, DEC_SEQ // Q_TILE),
        in_specs=[q, k, v, _cache_spec(layer, 0, True), _cache_spec(layer, 1, True), tab, tab,
                  pl.BlockSpec(memory_space=pltpu.SMEM)],
        out_specs=o,
        out_shape=jax.ShapeDtypeStruct((N_LAT_TOK, 256), BF16),
        compiler_params=_params(("arbitrary",) * 3, VMEM_LIMIT),
        name="lat_swa",
    )(p, p, p, cache, cache, cos, sin, sink)


def _lat_diff_kernel(q_ref, k_ref, v_ref, ck_ref, cv_ref, cos_ref, sin_ref, lp_ref, sg_ref, o_ref,
                     krot_ref, vb_ref, *, lam_init):
    t = pl.program_id(2)
    eighth = DIFF_QK_DIM // 4

    @pl.when(t == 0)
    def _():
        krot_ref[...] = _rope(k_ref[...], cos_ref[...], sin_ref[...], eighth).astype(BF16)
        vb_ref[...] = v_ref[...].astype(BF16)

    q0 = pl.multiple_of(t * Q_TILE, Q_TILE)
    q2 = _rope(q_ref[...], cos_ref[pl.ds(q0, Q_TILE), :], sin_ref[pl.ds(q0, Q_TILE), :], eighth)
    ckb = ck_ref[...].astype(BF16)
    cvb = cv_ref[...].astype(BF16)
    lam = _lambda(lp_ref[...], lam_init)
    scale = DIFF_QK_DIM ** -0.5
    lane = _lane(q2.shape)
    quarter = lane // DIFF_QK_DIM
    half = lane // HEAD_DIM
    outs = []
    for g in range(2):
        o = []
        for c in range(2):
            qm = jnp.where(quarter == 2 * g + c, q2, 0.0).astype(BF16)
            (e_lat, e_ctx), l = _exp_parts([_dot_nt(qm, krot_ref[...]) * scale, _dot_nt(qm, ckb) * scale])
            o.append((_dot(e_lat.astype(BF16), vb_ref[...]) + _dot(e_ctx.astype(BF16), cvb)) / l)
        outs.append(o[0] - lam * o[1])
    o2 = jnp.where(half == 0, outs[0], outs[1])
    o_ref[...] = _subln(o2, sg_ref[...], lam_init).astype(o_ref.dtype)


def _lat_diff(p, cache, lp, sg2, layer, lam_init):
    q, k, v, o = _lat_blocks((DIFF_Q, DIFF_K, DIFF_V))
    cos, sin = (jnp.asarray(a) for a in _rope_tables(DIFF_QK_DIM))
    tab = pl.BlockSpec((DEC_SEQ, LANES), lambda b, j, t: (0, 0))
    return pl.pallas_call(
        functools.partial(_lat_diff_kernel, lam_init=lam_init),
        grid=(DEC_BATCH, 2, DEC_SEQ // Q_TILE),
        in_specs=[q, k, v, _cache_spec(layer, 0, False), _cache_spec(layer, 1, False), tab, tab,
                  pl.BlockSpec((4, DIFF_QK_DIM), lambda b, j, t: (0, 0)),
                  pl.BlockSpec((1, LANES), lambda b, j, t: (0, 0))],
        out_specs=o,
        out_shape=jax.ShapeDtypeStruct((N_LAT_TOK, 256), BF16),
        scratch_shapes=[pltpu.VMEM((DEC_SEQ, LANES), BF16), pltpu.VMEM((DEC_SEQ, LANES), BF16)],
        compiler_params=_params(("arbitrary",) * 3, VMEM_LIMIT),
        name="lat_diff",
    )(p, p, p, cache, cache, cos, sin, lp, sg2)


def kernel(x_prompt, x_sample, cache_na_kv, cache_swa_kv, cache_diff_kv, c, c_ctx, norm1_g, norm2_g, ada_w,
           ada_b, w_in, na_rpb, swa_sink, diff_lambda, diff_subln_g, w_branch, w_gate, b_gate, w_o, w_ffn_in,
           w_ffn_out, final_norm_g):
    x = jnp.concatenate([x_prompt.reshape(N_CTX_TOK, D_MODEL), x_sample.reshape(N_LAT_TOK, D_MODEL)], axis=0)
    cond = jnp.zeros((COND_ROWS, D_MODEL), F32).at[0].set(c_ctx).at[1:1 + DEC_BATCH].set(c)
    mod = _adaln(cond, ada_w, ada_b).reshape(DEPTH, COND_ROWS, 6, D_MODEL)

    cache_na = cache_na_kv.reshape(DEC_BATCH, DEPTH, 2, PAST_LEN, 256)
    cache_swa = cache_swa_kv.reshape(DEC_BATCH, DEPTH, 2, PAST_LEN, LANES)
    cache_diff = cache_diff_kv.reshape(DEC_BATCH, DEPTH, 2, PAST_LEN, 256)
    fg = final_norm_g.reshape(1, D_MODEL)

    na_states, swa_states, diff_states = [], [], []
    for l in range(DEPTH):
        lam_init = 0.8 - 0.6 * math.exp(-0.3 * l)
        g1 = norm1_g[l].reshape(1, D_MODEL)
        sg2 = jnp.tile(diff_subln_g[l], 2).reshape(1, LANES)
        p = _proj(x, mod, l, g1, w_in[l].astype(BF16))

        pc = p[:N_CTX_TOK]
        na_states.append(jnp.moveaxis(pc[:, NA_K:NA_K + 512].reshape(BATCH, SEQ, 2, 4, HEAD_DIM), 2, 1))
        swa_states.append(jnp.moveaxis(pc[:, SWA_K:SWA_K + 256].reshape(BATCH, SEQ, 2, 2, HEAD_DIM), 2, 1))
        diff_states.append(jnp.moveaxis(pc[:, DIFF_K:DIFF_K + 512].reshape(BATCH, SEQ, 2, 4, HEAD_DIM), 2, 1))

        c_na, c_swa, c_diff = _ctx_mixers(p, swa_sink[l], diff_lambda[l], sg2, lam_init)
        c_f = _fourier(p, SEQ, BATCH, 0)
        l_na = _lat_na(p, cache_na, _na_bias_table(na_rpb[l]), l)
        l_swa = _lat_swa(p, cache_swa, swa_sink[l], l)
        l_f = _fourier(p, DEC_SEQ, DEC_BATCH, N_CTX_TOK)
        l_diff = _lat_diff(p, cache_diff, diff_lambda[l], sg2, l, lam_init)
        branches = [jnp.concatenate(pair, axis=0)
                    for pair in ((c_na, l_na), (c_swa, l_swa), (c_f, l_f), (c_diff, l_diff))]

        x = _merge(x, mod, l, g1, branches, w_gate[l].astype(BF16), b_gate[l].reshape(1, -1),
                   w_branch[l].astype(BF16), w_o[l].astype(BF16))
        x = _ffn(x, mod, l, norm2_g[l].reshape(1, D_MODEL), w_ffn_in[l].astype(BF16),
                 w_ffn_out[l].astype(BF16), fg, final=(l == DEPTH - 1))

    y_prompt = x[:N_CTX_TOK].reshape(BATCH, SEQ, D_MODEL)
    y_sample = x[N_CTX_TOK:].reshape(DEC_BATCH, DEC_SEQ, D_MODEL)
    return (y_prompt, y_sample, jnp.stack(na_states, axis=1), jnp.stack(swa_states, axis=1),
            jnp.stack(diff_states, axis=1))
```

```python
import functools
import math

import numpy as np
import jax
import jax.numpy as jnp
from jax import lax
from jax.experimental import pallas as pl
from jax.experimental.pallas import tpu as pltpu

D_MODEL = 1024
BATCH = 16
SEQ = 256
DEPTH = 2
DEC_BATCH = 4
DEC_SEQ = 2048
PAST_LEN = 512
GRID_W = 64
GRID_H = DEC_SEQ // GRID_W
HEAD_DIM = 64
NA_WIN_H = 8
NA_WIN_W = 16
SWA_WINDOW = 128
DIFF_QK_DIM = 32
D_FF = 2816
D_IN = 2304
ROPE_BASE = 10000.0
NORM_EPS = 1e-6
NEG_INF = -1e30
LOG2E = math.log2(math.e)

NA_Q, NA_K, NA_V = 0, 256, 512
SWA_Q, SWA_K, SWA_V = 768, 1024, 1152
FNET_U = 1280
DIFF_Q, DIFF_K, DIFF_V = 1536, 1792, 2048

LANES = 128
N_CTX_TOK = BATCH * SEQ
N_LAT_TOK = DEC_BATCH * DEC_SEQ
N_TOK = N_CTX_TOK + N_LAT_TOK
COND_ROWS = 8
TOKEN_TILE = 256
Q_TILE = 256
NA_TILE_ROWS = Q_TILE // GRID_W
NA_BAND_ROWS = NA_WIN_H + NA_TILE_ROWS
SWA_KEYS = 2 * Q_TILE
VMEM_LIMIT = 56 * 1024 * 1024

F32 = jnp.float32
BF16 = jnp.bfloat16


def _params(semantics, vmem=None):
    return pltpu.CompilerParams(dimension_semantics=semantics, vmem_limit_bytes=vmem)


def _dot(a, b):
    return jnp.dot(a, b, preferred_element_type=F32)


def _dot_nt(a, b):
    return lax.dot_general(a, b, (((1,), (1,)), ((), ())), preferred_element_type=F32)


def _split(x):
    hi = x.astype(BF16)
    lo = (x - hi.astype(F32)).astype(BF16)
    return hi, lo


def _dot3(a_hi, a_lo, b_hi, b_lo):
    return _dot(a_hi, b_hi) + _dot(a_lo, b_hi) + _dot(a_hi, b_lo)


def _sigmoid(x):
    return 1.0 / (1.0 + jnp.exp(-x))


def _rms(x, g):
    return x * lax.rsqrt(jnp.mean(x * x, axis=-1, keepdims=True) + NORM_EPS) * g


def _exp_parts(blocks, extra=None):
    m = None
    for s in blocks:
        mi = jnp.max(s, axis=-1, keepdims=True)
        m = mi if m is None else jnp.maximum(m, mi)
    if extra is not None:
        m = jnp.maximum(m, extra)
    es = [jnp.exp2(s - m) for s in blocks]
    l = None
    for e in es:
        li = jnp.sum(e, axis=-1, keepdims=True)
        l = li if l is None else l + li
    if extra is not None:
        l = l + jnp.exp2(extra - m)
    return es, l


def _lane(shape):
    return lax.broadcasted_iota(jnp.int32, shape, 1)


def _rope(x, cos, sin_signed, half):
    lane = _lane(x.shape)
    partner = jnp.where((lane % (2 * half)) < half,
                        pltpu.roll(x, LANES - half, 1), pltpu.roll(x, half, 1))
    return x * cos + partner * sin_signed


def _lambda(lp, lam_init):
    s1 = jnp.sum(lp[0:1, :] * lp[1:2, :], axis=-1, keepdims=True)
    s2 = jnp.sum(lp[2:3, :] * lp[3:4, :], axis=-1, keepdims=True)
    return jnp.exp(s1) - jnp.exp(s2) + lam_init


def _subln(o, g2, lam_init):
    lane = _lane(o.shape)
    sq = o * o
    ms0 = jnp.sum(jnp.where(lane < HEAD_DIM, sq, 0.0), axis=-1, keepdims=True)
    ms1 = jnp.sum(jnp.where(lane >= HEAD_DIM, sq, 0.0), axis=-1, keepdims=True)
    ms = jnp.where(lane < HEAD_DIM, ms0, ms1) * (1.0 / HEAD_DIM)
    return (o * lax.rsqrt(ms + NORM_EPS) * g2) * (1.0 - lam_init)


def _adaln_kernel(cond_ref, w_ref, b_ref, o_ref):
    c = cond_ref[...]
    s = c * _sigmoid(c)
    s_hi, s_lo = _split(s)
    w_hi, w_lo = _split(w_ref[...])
    o_ref[...] = _dot3(s_hi, s_lo, w_hi, w_lo) + b_ref[...]


def _adaln(cond, ada_w, ada_b):
    tn = 1536
    n = 6 * D_MODEL
    return pl.pallas_call(
        _adaln_kernel,
        grid=(DEPTH, n // tn),
        in_specs=[
            pl.BlockSpec((COND_ROWS, D_MODEL), lambda l, j: (0, 0)),
            pl.BlockSpec((None, D_MODEL, tn), lambda l, j: (l, 0, j)),
            pl.BlockSpec((None, 1, tn), lambda l, j: (l, 0, j)),
        ],
        out_specs=pl.BlockSpec((None, COND_ROWS, tn), lambda l, j: (l, 0, j)),
        out_shape=jax.ShapeDtypeStruct((DEPTH, COND_ROWS, n), F32),
        compiler_params=_params(("arbitrary", "arbitrary")),
        name="adaln",
    )(cond, ada_w, ada_b.reshape(DEPTH, 1, n))


def _cond_row(i):
    n_ctx = N_CTX_TOK // TOKEN_TILE
    per_seq = DEC_SEQ // TOKEN_TILE
    return jnp.where(i < n_ctx, 0, 1 + (i - n_ctx) // per_seq)


def _mod_spec(layer):
    return pl.BlockSpec((None, None, 6, D_MODEL), lambda i: (layer, _cond_row(i), 0, 0))


def _tok_spec(width):
    return pl.BlockSpec((TOKEN_TILE, width), lambda i: (i, 0))


def _const_spec(shape):
    nd = len(shape)
    return pl.BlockSpec(shape, lambda i: (0,) * nd)


def _proj_kernel(x_ref, mod_ref, g_ref, w_ref, p_ref, na_ref, swa_ref, diff_ref):
    h = _rms(x_ref[...], g_ref[...]) * (1.0 + mod_ref[1:2, :]) + mod_ref[0:1, :]
    p = _dot(h.astype(BF16), w_ref[...])
    p_ref[...] = p

    @pl.when(pl.program_id(0) < N_CTX_TOK // TOKEN_TILE)
    def _():
        for s in range(TOKEN_TILE // SEQ):
            rows = slice(s * SEQ, (s + 1) * SEQ)
            for which in range(2):
                na_ref[s, which] = p[rows, NA_K + 256 * which: NA_K + 256 * (which + 1)]
                swa_ref[s, which] = p[rows, SWA_K + LANES * which: SWA_K + LANES * (which + 1)]
                diff_ref[s, which] = p[rows, DIFF_K + 256 * which: DIFF_K + 256 * (which + 1)]


def _proj(x, mod, layer, g, w):
    seqs = TOKEN_TILE // SEQ
    last = N_CTX_TOK // TOKEN_TILE - 1

    def state_spec(width):
        return pl.BlockSpec((seqs, 2, SEQ, width), lambda i: (jnp.minimum(i, last), 0, 0, 0))

    def state_shape(width):
        return jax.ShapeDtypeStruct((BATCH, 2, SEQ, width), F32)

    return pl.pallas_call(
        _proj_kernel,
        grid=(N_TOK // TOKEN_TILE,),
        in_specs=[_tok_spec(D_MODEL), _mod_spec(layer), _const_spec((1, D_MODEL)),
                  _const_spec((D_MODEL, D_IN))],
        out_specs=[_tok_spec(D_IN), state_spec(256), state_spec(LANES), state_spec(256)],
        out_shape=[jax.ShapeDtypeStruct((N_TOK, D_IN), F32), state_shape(256), state_shape(LANES),
                   state_shape(256)],
        compiler_params=_params(("arbitrary",), VMEM_LIMIT),
        name="proj",
    )(x, mod, g, w)


N_BRANCH = 4
BRANCH_DIM = D_MODEL // N_BRANCH


def _merge_kernel(x_ref, mod_ref, g_ref, *refs):
    ctx_refs = refs[:N_BRANCH]
    lat_refs = refs[N_BRANCH:2 * N_BRANCH]
    wg_ref, bg_ref, wb_ref, wo_ref, o_ref = refs[2 * N_BRANCH:]
    is_ctx = pl.program_id(0) < N_CTX_TOK // TOKEN_TILE
    x = x_ref[...]
    h = _rms(x, g_ref[...]) * (1.0 + mod_ref[1:2, :]) + mod_ref[0:1, :]
    hb = h.astype(BF16)
    merged = None
    for k in range(N_BRANCH):
        cols = slice(k * D_MODEL, (k + 1) * D_MODEL)
        gate = _sigmoid(_dot(hb, wg_ref[:, cols]) + bg_ref[:, cols])
        branch = jnp.where(is_ctx, ctx_refs[k][...], lat_refs[k][...])
        term = gate * _dot(branch, wb_ref[k])
        merged = term if merged is None else merged + term
    o_ref[...] = x + mod_ref[2:3, :] * _dot(merged.astype(BF16), wo_ref[...])


def _merge(x, mod, layer, g, ctx_branches, lat_branches, wg, bg, wb, wo):
    n_ctx = N_CTX_TOK // TOKEN_TILE
    n_lat = N_LAT_TOK // TOKEN_TILE
    ctx_spec = pl.BlockSpec((TOKEN_TILE, BRANCH_DIM), lambda i: (jnp.minimum(i, n_ctx - 1), 0))
    lat_spec = pl.BlockSpec((TOKEN_TILE, BRANCH_DIM), lambda i: (jnp.clip(i - n_ctx, 0, n_lat - 1), 0))
    return pl.pallas_call(
        _merge_kernel,
        grid=(N_TOK // TOKEN_TILE,),
        in_specs=[_tok_spec(D_MODEL), _mod_spec(layer), _const_spec((1, D_MODEL))]
                 + [ctx_spec] * N_BRANCH + [lat_spec] * N_BRANCH
                 + [_const_spec((D_MODEL, N_BRANCH * D_MODEL)), _const_spec((1, N_BRANCH * D_MODEL)),
                    _const_spec((N_BRANCH, BRANCH_DIM, D_MODEL)), _const_spec((D_MODEL, D_MODEL))],
        out_specs=_tok_spec(D_MODEL),
        out_shape=jax.ShapeDtypeStruct((N_TOK, D_MODEL), F32),
        compiler_params=_params(("arbitrary",), VMEM_LIMIT),
        name="merge",
    )(x, mod, g, *ctx_branches, *lat_branches, wg, bg, wb, wo)


def _ffn_kernel(x_ref, mod_ref, g_ref, wi_ref, wo_ref, fg_ref, o_ref, *, final):
    x = x_ref[...]
    h = _rms(x, g_ref[...]) * (1.0 + mod_ref[4:5, :]) + mod_ref[3:4, :]
    hb = h.astype(BF16)
    a = _dot(hb, wi_ref[:, :D_FF])
    b = _dot(hb, wi_ref[:, D_FF:])
    f = (a * _sigmoid(a)) * b
    y = x + mod_ref[5:6, :] * _dot(f.astype(BF16), wo_ref[...])
    o_ref[...] = _rms(y, fg_ref[...]) if final else y


def _ffn(x, mod, layer, g, wi, wo, fg, final):
    return pl.pallas_call(
        functools.partial(_ffn_kernel, final=final),
        grid=(N_TOK // TOKEN_TILE,),
        in_specs=[_tok_spec(D_MODEL), _mod_spec(layer), _const_spec((1, D_MODEL)),
                  _const_spec((D_MODEL, 2 * D_FF)), _const_spec((D_FF, D_MODEL)),
                  _const_spec((1, D_MODEL))],
        out_specs=_tok_spec(D_MODEL),
        out_shape=jax.ShapeDtypeStruct((N_TOK, D_MODEL), F32),
        compiler_params=_params(("arbitrary",), VMEM_LIMIT),
        name="ffn",
    )(x, mod, g, wi, wo, fg)


@functools.lru_cache(maxsize=None)
def _dft_tables(n, blocks):
    k = np.arange(n, dtype=np.int64)
    ang = 2.0 * np.pi * ((k[:, None] * k[None, :]) % n).astype(np.float64) / n
    out = []
    for m in (np.cos(ang), np.sin(ang)):
        m = np.kron(np.eye(blocks), m / math.sqrt(n)).astype(np.float32)
        hi = m.astype(BF16)
        lo = (m - hi.astype(np.float32)).astype(BF16)
        out += [hi, lo]
    return tuple(out)


def _fourier_kernel(u_ref, ch_ref, cl_ref, sh_ref, sl_ref, cch_ref, ccl_ref, sch_ref, scl_ref, o_ref):
    u_hi, u_lo = _split(u_ref[...])
    a_hi, a_lo = _split(_dot3(ch_ref[...], cl_ref[...], u_hi, u_lo))
    b_hi, b_lo = _split(_dot3(sh_ref[...], sl_ref[...], u_hi, u_lo))
    y = _dot3(a_hi, a_lo, cch_ref[...], ccl_ref[...]) - _dot3(b_hi, b_lo, sch_ref[...], scl_ref[...])
    o_ref[...] = y.astype(o_ref.dtype)


def _fourier(p, seq, n_seq, first_tok):
    tf = 256
    nf = seq // tf
    width = 256
    first_blk = first_tok // seq
    pos = [jnp.asarray(t) for t in _dft_tables(seq, 1)]
    chan = [jnp.asarray(t) for t in _dft_tables(width // 4, 4)]
    pos_spec = pl.BlockSpec((tf, seq), lambda f, b: (f, 0))
    chan_spec = pl.BlockSpec((width, width), lambda f, b: (0, 0))
    return pl.pallas_call(
        _fourier_kernel,
        grid=(nf, n_seq),
        in_specs=[pl.BlockSpec((seq, width), lambda f, b: (first_blk + b, FNET_U // width))]
                 + [pos_spec] * 4 + [chan_spec] * 4,
        out_specs=pl.BlockSpec((tf, width), lambda f, b: (b * nf + f, 0)),
        out_shape=jax.ShapeDtypeStruct((n_seq * seq, width), BF16),
        compiler_params=_params(("arbitrary", "arbitrary"), VMEM_LIMIT),
        name="fourier",
    )(p, *pos, *chan)


def _ctx_mixer_kernel(p_ref, sink_ref, lp_ref, sg_ref, na_ref, swa_ref, diff_ref, *, lam_init):
    shape = (SEQ, LANES)
    lane = _lane(shape)
    half = lane // HEAD_DIM
    quarter = lane // DIFF_QK_DIM
    scale = HEAD_DIM ** -0.5 * LOG2E

    def pair(col, j):
        return p_ref[:, col + LANES * j: col + LANES * (j + 1)]

    for j in range(2):
        q2 = pair(NA_Q, j) * scale
        kb = pair(NA_K, j).astype(BF16)
        vb = pair(NA_V, j).astype(BF16)
        outs = []
        for g in range(2):
            qm = jnp.where(half == g, q2, 0.0).astype(BF16)
            (e,), l = _exp_parts([_dot_nt(qm, kb)])
            outs.append(_dot(e.astype(BF16), vb) / l)
        na_ref[:, LANES * j: LANES * (j + 1)] = jnp.where(half == 0, outs[0], outs[1]).astype(na_ref.dtype)

    k2 = p_ref[:, SWA_K: SWA_K + LANES]
    v2 = p_ref[:, SWA_V: SWA_V + LANES]
    k2s = pltpu.roll(k2, HEAD_DIM, 1)
    v2s = pltpu.roll(v2, HEAD_DIM, 1)
    for j in range(2):
        kb = jnp.where(half == j, k2, k2s).astype(BF16)
        vb = jnp.where(half == j, v2, v2s).astype(BF16)
        q2 = pair(SWA_Q, j) * scale
        outs = []
        for g in range(2):
            qm = jnp.where(half == g, q2, 0.0).astype(BF16)
            (e,), l = _exp_parts([_dot_nt(qm, kb)], extra=sink_ref[2 * j + g] * LOG2E)
            outs.append(_dot(e.astype(BF16), vb) / l)
        swa_ref[:, LANES * j: LANES * (j + 1)] = jnp.where(half == 0, outs[0], outs[1]).astype(swa_ref.dtype)

    lam = _lambda(lp_ref[...], lam_init)
    for j in range(2):
        q2 = pair(DIFF_Q, j) * (DIFF_QK_DIM ** -0.5 * LOG2E)
        kb = pair(DIFF_K, j).astype(BF16)
        vb = pair(DIFF_V, j).astype(BF16)
        outs = []
        for g in range(2):
            o = []
            for c in range(2):
                qm = jnp.where(quarter == 2 * g + c, q2, 0.0).astype(BF16)
                (e,), l = _exp_parts([_dot_nt(qm, kb)])
                o.append(_dot(e.astype(BF16), vb) / l)
            outs.append(o[0] - lam * o[1])
        o2 = jnp.where(half == 0, outs[0], outs[1])
        diff_ref[:, LANES * j: LANES * (j + 1)] = _subln(o2, sg_ref[...], lam_init).astype(diff_ref.dtype)


def _ctx_mixers(p, sink, lp, sg2, lam_init):
    out = jax.ShapeDtypeStruct((N_CTX_TOK, 256), BF16)
    ospec = pl.BlockSpec((SEQ, 256), lambda b: (b, 0))
    return pl.pallas_call(
        functools.partial(_ctx_mixer_kernel, lam_init=lam_init),
        grid=(BATCH,),
        in_specs=[pl.BlockSpec((SEQ, D_IN), lambda b: (b, 0)),
                  pl.BlockSpec(memory_space=pltpu.SMEM),
                  pl.BlockSpec((4, DIFF_QK_DIM), lambda b: (0, 0)),
                  pl.BlockSpec((1, LANES), lambda b: (0, 0))],
        out_specs=[ospec, ospec, ospec],
        out_shape=[out, out, out],
        compiler_params=_params(("arbitrary",), VMEM_LIMIT),
        name="ctx_mixers",
    )(p, sink, lp, sg2)


N_DR = 2 * NA_WIN_H - 1
N_DC = 2 * NA_WIN_W - 1
N_PAIR_BLOCKS = N_DR + 1


def _na_band_start(t):
    return jnp.clip(NA_TILE_ROWS * t - NA_WIN_H // 2, 0, GRID_H - NA_BAND_ROWS)


def _na_bias_rows(rpb):
    rows = jnp.pad(rpb, ((0, 0), (1, 1), (0, GRID_W - N_DC)))
    return jnp.concatenate([rows[:, :N_PAIR_BLOCKS], rows[:, 1:]], axis=-1)


def _lat_na_kernel(q_ref, k_ref, v_ref, ck_ref, cv_ref, rows_ref, o_ref, pair_ref):
    t = pl.program_id(2)
    blk = (GRID_W, LANES)

    @pl.when(t == 0)
    def _():
        c = lax.broadcasted_iota(jnp.int32, blk, 0)
        kc = _lane(blk) % GRID_W
        cs = jnp.clip(c - NA_WIN_W // 2, 0, GRID_W - NA_WIN_W)
        in_cols = (kc >= cs) & (kc < cs + NA_WIN_W)
        for g in range(2):
            for i in range(N_PAIR_BLOCKS):
                row = jnp.broadcast_to(rows_ref[g, i:i + 1, :], blk) * LOG2E
                toeplitz = pltpu.roll(row, LANES - (NA_WIN_W - 1), 1, stride=1, stride_axis=0)
                pair_ref[g, i] = jnp.where(in_cols, toeplitz, NEG_INF)

    band = _na_band_start(t)
    start = pl.multiple_of(band * GRID_W, GRID_W)
    nk = NA_BAND_ROWS * GRID_W
    kb = k_ref[pl.ds(start, nk), :].astype(BF16)
    vb = v_ref[pl.ds(start, nk), :].astype(BF16)
    ckb = ck_ref[...].astype(BF16)
    cvb = cv_ref[...].astype(BF16)
    q2 = q_ref[...] * (HEAD_DIM ** -0.5 * LOG2E)
    half = _lane(q2.shape) // HEAD_DIM

    def bias(g):
        rows = []
        for ri in range(NA_TILE_ROWS):
            r = NA_TILE_ROWS * t + ri
            first = jnp.clip(r - NA_WIN_H // 2, 0, GRID_H - NA_WIN_H) - band
            d0 = band - r + NA_WIN_H - 1
            blocks = []
            for m in range(NA_BAND_ROWS // 2):
                idx = jnp.clip(d0 + 2 * m + 1, 0, N_PAIR_BLOCKS - 1)
                jrow = 2 * m + _lane(blk) // GRID_W
                in_rows = (jrow >= first) & (jrow < first + NA_WIN_H)
                blocks.append(jnp.where(in_rows, pair_ref[g, idx], NEG_INF))
            rows.append(jnp.concatenate(blocks, axis=1))
        return jnp.concatenate(rows, axis=0)

    outs = []
    for g in range(2):
        qm = jnp.where(half == g, q2, 0.0).astype(BF16)
        (e_loc, e_ctx), l = _exp_parts([_dot_nt(qm, kb) + bias(g), _dot_nt(qm, ckb)])
        outs.append((_dot(e_loc.astype(BF16), vb) + _dot(e_ctx.astype(BF16), cvb)) / l)
    o_ref[...] = jnp.where(half == 0, outs[0], outs[1]).astype(o_ref.dtype)


def _lat_blocks(col):
    qt = DEC_SEQ // Q_TILE
    first_q = N_CTX_TOK // Q_TILE
    first_k = N_CTX_TOK // DEC_SEQ
    q = pl.BlockSpec((Q_TILE, LANES), lambda b, j, t: (first_q + qt * b + t, col[0] // LANES + j))
    k = pl.BlockSpec((DEC_SEQ, LANES), lambda b, j, t: (first_k + b, col[1] // LANES + j))
    v = pl.BlockSpec((DEC_SEQ, LANES), lambda b, j, t: (first_k + b, col[2] // LANES + j))
    o = pl.BlockSpec((Q_TILE, LANES), lambda b, j, t: (qt * b + t, j))
    return q, k, v, o


def _cache_spec(layer, which, shared_kv):
    return pl.BlockSpec((None, None, None, PAST_LEN, LANES),
                        lambda b, j, t: (b, layer, which, 0, 0 if shared_kv else j))


def _lat_na(p, cache, bias_rows, layer):
    q, k, v, o = _lat_blocks((NA_Q, NA_K, NA_V))
    rows_spec = pl.BlockSpec((2, N_PAIR_BLOCKS, LANES), lambda b, j, t: (j, 0, 0))
    return pl.pallas_call(
        _lat_na_kernel,
        grid=(DEC_BATCH, 2, DEC_SEQ // Q_TILE),
        in_specs=[q, k, v, _cache_spec(layer, 0, False), _cache_spec(layer, 1, False), rows_spec],
        out_specs=o,
        out_shape=jax.ShapeDtypeStruct((N_LAT_TOK, 256), BF16),
        scratch_shapes=[pltpu.VMEM((2, N_PAIR_BLOCKS, GRID_W, LANES), F32)],
        compiler_params=_params(("arbitrary",) * 3, VMEM_LIMIT),
        name="lat_na",
    )(p, p, p, cache, cache, bias_rows)


@functools.lru_cache(maxsize=None)
def _rope_tables(dim):
    quarter = dim // 4
    pos = np.arange(DEC_SEQ)
    rows, cols = pos // GRID_W, pos % GRID_W
    lane = np.arange(LANES)
    w = lane % dim
    axis_pos = np.where((w // (dim // 2) == 0)[None, :], rows[:, None], cols[:, None]).astype(np.float64)
    u = w % (dim // 2)
    inv = ROPE_BASE ** (-(u % quarter).astype(np.float64) * 2.0 / (dim // 2))
    ang = axis_pos * inv[None, :]
    sign = np.where(u < quarter, -1.0, 1.0)[None, :]
    return np.cos(ang).astype(np.float32), (np.sin(ang) * sign).astype(np.float32)


def _swa_key_start(t):
    return jnp.clip(Q_TILE * t - SWA_WINDOW, 0, DEC_SEQ - SWA_KEYS)


def _lat_swa_kernel(q_ref, k_ref, v_ref, ck_ref, cv_ref, cos_ref, sin_ref, sink_ref, o_ref):
    j = pl.program_id(1)
    t = pl.program_id(2)
    q0 = pl.multiple_of(t * Q_TILE, Q_TILE)
    k0 = pl.multiple_of(_swa_key_start(t), SWA_WINDOW)
    quarter = HEAD_DIM // 4
    q2 = _rope(q_ref[...], cos_ref[pl.ds(q0, Q_TILE), :], sin_ref[pl.ds(q0, Q_TILE), :], quarter)
    q2 = q2 * (HEAD_DIM ** -0.5 * LOG2E)
    k2 = _rope(k_ref[pl.ds(k0, SWA_KEYS), :], cos_ref[pl.ds(k0, SWA_KEYS), :], sin_ref[pl.ds(k0, SWA_KEYS), :],
               quarter)
    v2 = v_ref[pl.ds(k0, SWA_KEYS), :]
    ck2 = ck_ref[...]
    cv2 = cv_ref[...]

    def head_j(x):
        h = _lane(x.shape) // HEAD_DIM
        return jnp.where(h == j, x, pltpu.roll(x, HEAD_DIM, 1)).astype(BF16)

    kb, vb, ckb, cvb = head_j(k2), head_j(v2), head_j(ck2), head_j(cv2)
    qpos = q0 + lax.broadcasted_iota(jnp.int32, (Q_TILE, SWA_KEYS), 0)
    kpos = k0 + lax.broadcasted_iota(jnp.int32, (Q_TILE, SWA_KEYS), 1)
    valid = jnp.abs(kpos - qpos) <= SWA_WINDOW
    half = _lane(q2.shape) // HEAD_DIM
    outs = []
    for g in range(2):
        qm = jnp.where(half == g, q2, 0.0).astype(BF16)
        s_loc = jnp.where(valid, _dot_nt(qm, kb), NEG_INF)
        (e_loc, e_ctx), l = _exp_parts([s_loc, _dot_nt(qm, ckb)], extra=sink_ref[2 * j + g] * LOG2E)
        outs.append((_dot(e_loc.astype(BF16), vb) + _dot(e_ctx.astype(BF16), cvb)) / l)
    o_ref[...] = jnp.where(half == 0, outs[0], outs[1]).astype(o_ref.dtype)


def _lat_swa(p, cache, sink, layer):
    q, _, _, o = _lat_blocks((SWA_Q, SWA_K, SWA_V))
    first_k = N_CTX_TOK // DEC_SEQ
    k = pl.BlockSpec((DEC_SEQ, LANES), lambda b, j, t: (first_k + b, SWA_K // LANES))
    v = pl.BlockSpec((DEC_SEQ, LANES), lambda b, j, t: (first_k + b, SWA_V // LANES))
    cos, sin = (jnp.asarray(a) for a in _rope_tables(HEAD_DIM))
    tab = pl.BlockSpec((DEC_SEQ, LANES), lambda b, j, t: (0, 0))
    return pl.pallas_call(
        _lat_swa_kernel,
        grid=(DEC_BATCH, 2, DEC_SEQ // Q_TILE),
        in_specs=[q, k, v, _cache_spec(layer, 0, True), _cache_spec(layer, 1, True), tab, tab,
                  pl.BlockSpec(memory_space=pltpu.SMEM)],
        out_specs=o,
        out_shape=jax.ShapeDtypeStruct((N_LAT_TOK, 256), BF16),
        compiler_params=_params(("arbitrary",) * 3, VMEM_LIMIT),
        name="lat_swa",
    )(p, p, p, cache, cache, cos, sin, sink)


def _lat_diff_kernel(q_ref, k_ref, v_ref, ck_ref, cv_ref, cos_ref, sin_ref, lp_ref, sg_ref, o_ref,
                     krot_ref, vb_ref, *, lam_init):
    t = pl.program_id(2)
    eighth = DIFF_QK_DIM // 4

    @pl.when(t == 0)
    def _():
        krot_ref[...] = _rope(k_ref[...], cos_ref[...], sin_ref[...], eighth).astype(BF16)
        vb_ref[...] = v_ref[...].astype(BF16)

    q0 = pl.multiple_of(t * Q_TILE, Q_TILE)
    q2 = _rope(q_ref[...], cos_ref[pl.ds(q0, Q_TILE), :], sin_ref[pl.ds(q0, Q_TILE), :], eighth)
    q2 = q2 * (DIFF_QK_DIM ** -0.5 * LOG2E)
    ckb = ck_ref[...].astype(BF16)
    cvb = cv_ref[...].astype(BF16)
    lam = _lambda(lp_ref[...], lam_init)
    lane = _lane(q2.shape)
    quarter = lane // DIFF_QK_DIM
    half = lane // HEAD_DIM
    outs = []
    for g in range(2):
        o = []
        for c in range(2):
            qm = jnp.where(quarter == 2 * g + c, q2, 0.0).astype(BF16)
            (e_lat, e_ctx), l = _exp_parts([_dot_nt(qm, krot_ref[...]), _dot_nt(qm, ckb)])
            o.append((_dot(e_lat.astype(BF16), vb_ref[...]) + _dot(e_ctx.astype(BF16), cvb)) / l)
        outs.append(o[0] - lam * o[1])
    o2 = jnp.where(half == 0, outs[0], outs[1])
    o_ref[...] = _subln(o2, sg_ref[...], lam_init).astype(o_ref.dtype)


def _lat_diff(p, cache, lp, sg2, layer, lam_init):
    q, k, v, o = _lat_blocks((DIFF_Q, DIFF_K, DIFF_V))
    cos, sin = (jnp.asarray(a) for a in _rope_tables(DIFF_QK_DIM))
    tab = pl.BlockSpec((DEC_SEQ, LANES), lambda b, j, t: (0, 0))
    return pl.pallas_call(
        functools.partial(_lat_diff_kernel, lam_init=lam_init),
        grid=(DEC_BATCH, 2, DEC_SEQ // Q_TILE),
        in_specs=[q, k, v, _cache_spec(layer, 0, False), _cache_spec(layer, 1, False), tab, tab,
                  pl.BlockSpec((4, DIFF_QK_DIM), lambda b, j, t: (0, 0)),
                  pl.BlockSpec((1, LANES), lambda b, j, t: (0, 0))],
        out_specs=o,
        out_shape=jax.ShapeDtypeStruct((N_LAT_TOK, 256), BF16),
        scratch_shapes=[pltpu.VMEM((DEC_SEQ, LANES), BF16), pltpu.VMEM((DEC_SEQ, LANES), BF16)],
        compiler_params=_params(("arbitrary",) * 3, VMEM_LIMIT),
        name="lat_diff",
    )(p, p, p, cache, cache, cos, sin, lp, sg2)


def kernel(x_prompt, x_sample, cache_na_kv, cache_swa_kv, cache_diff_kv, c, c_ctx, norm1_g, norm2_g, ada_w,
           ada_b, w_in, na_rpb, swa_sink, diff_lambda, diff_subln_g, w_branch, w_gate, b_gate, w_o, w_ffn_in,
           w_ffn_out, final_norm_g):
    x = jnp.concatenate([x_prompt.reshape(N_CTX_TOK, D_MODEL), x_sample.reshape(N_LAT_TOK, D_MODEL)], axis=0)
    cond = jnp.zeros((COND_ROWS, D_MODEL), F32).at[0].set(c_ctx).at[1:1 + DEC_BATCH].set(c)
    mod = _adaln(cond, ada_w, ada_b).reshape(DEPTH, COND_ROWS, 6, D_MODEL)

    cache_na = cache_na_kv.reshape(DEC_BATCH, DEPTH, 2, PAST_LEN, 256)
    cache_swa = cache_swa_kv.reshape(DEC_BATCH, DEPTH, 2, PAST_LEN, LANES)
    cache_diff = cache_diff_kv.reshape(DEC_BATCH, DEPTH, 2, PAST_LEN, 256)
    fg = final_norm_g.reshape(1, D_MODEL)

    na_states, swa_states, diff_states = [], [], []
    for l in range(DEPTH):
        lam_init = 0.8 - 0.6 * math.exp(-0.3 * l)
        g1 = norm1_g[l].reshape(1, D_MODEL)
        sg2 = jnp.tile(diff_subln_g[l], 2).reshape(1, LANES)
        p, na_kv, swa_kv, diff_kv = _proj(x, mod, l, g1, w_in[l].astype(BF16))
        na_states.append(na_kv.reshape(BATCH, 2, SEQ, 4, HEAD_DIM))
        swa_states.append(swa_kv.reshape(BATCH, 2, SEQ, 2, HEAD_DIM))
        diff_states.append(diff_kv.reshape(BATCH, 2, SEQ, 4, HEAD_DIM))

        c_na, c_swa, c_diff = _ctx_mixers(p, swa_sink[l], diff_lambda[l], sg2, lam_init)
        c_f = _fourier(p, SEQ, BATCH, 0)
        l_na = _lat_na(p, cache_na, _na_bias_rows(na_rpb[l]), l)
        l_swa = _lat_swa(p, cache_swa, swa_sink[l], l)
        l_f = _fourier(p, DEC_SEQ, DEC_BATCH, N_CTX_TOK)
        l_diff = _lat_diff(p, cache_diff, diff_lambda[l], sg2, l, lam_init)

        x = _merge(x, mod, l, g1, (c_na, c_swa, c_f, c_diff), (l_na, l_swa, l_f, l_diff),
                   w_gate[l].astype(BF16), b_gate[l].reshape(1, -1), w_branch[l].astype(BF16),
                   w_o[l].astype(BF16))
        x = _ffn(x, mod, l, norm2_g[l].reshape(1, D_MODEL), w_ffn_in[l].astype(BF16),
                 w_ffn_out[l].astype(BF16), fg, final=(l == DEPTH - 1))

    y_prompt = x[:N_CTX_TOK].reshape(BATCH, SEQ, D_MODEL)
    y_sample = x[N_CTX_TOK:].reshape(DEC_BATCH, DEC_SEQ, D_MODEL)
    return (y_prompt, y_sample, jnp.stack(na_states, axis=1), jnp.stack(swa_states, axis=1),
            jnp.stack(diff_states, axis=1))
```

```python
import functools
import math

import numpy as np
import jax
import jax.numpy as jnp
from jax import lax
from jax.experimental import pallas as pl
from jax.experimental.pallas import tpu as pltpu

D_MODEL = 1024
BATCH = 16
SEQ = 256
DEPTH = 2
DEC_BATCH = 4
DEC_SEQ = 2048
PAST_LEN = 512
GRID_W = 64
GRID_H = DEC_SEQ // GRID_W
HEAD_DIM = 64
NA_WIN_H = 8
NA_WIN_W = 16
SWA_WINDOW = 128
DIFF_QK_DIM = 32
D_FF = 2816
D_IN = 2304
ROPE_BASE = 10000.0
NORM_EPS = 1e-6
NEG_INF = -1e30
LOG2E = math.log2(math.e)

NA_Q, NA_K, NA_V = 0, 256, 512
SWA_Q, SWA_K, SWA_V = 768, 1024, 1152
FNET_U = 1280
DIFF_Q, DIFF_K, DIFF_V = 1536, 1792, 2048

LANES = 128
N_CTX_TOK = BATCH * SEQ
N_LAT_TOK = DEC_BATCH * DEC_SEQ
N_TOK = N_CTX_TOK + N_LAT_TOK
COND_ROWS = 8
TOKEN_TILE = 512
Q_TILE = 256
NA_TILE_ROWS = Q_TILE // GRID_W
NA_BAND_ROWS = NA_WIN_H + NA_TILE_ROWS
SWA_KEYS = 2 * Q_TILE
VMEM_LIMIT = 56 * 1024 * 1024

F32 = jnp.float32
BF16 = jnp.bfloat16


def _params(semantics, vmem=None):
    return pltpu.CompilerParams(dimension_semantics=semantics, vmem_limit_bytes=vmem)


def _dot(a, b):
    return jnp.dot(a, b, preferred_element_type=F32)


def _dot_nt(a, b):
    return lax.dot_general(a, b, (((1,), (1,)), ((), ())), preferred_element_type=F32)


def _split(x):
    hi = x.astype(BF16)
    lo = (x - hi.astype(F32)).astype(BF16)
    return hi, lo


def _dot3(a_hi, a_lo, b_hi, b_lo):
    return _dot(a_hi, b_hi) + _dot(a_lo, b_hi) + _dot(a_hi, b_lo)


def _sigmoid(x):
    return 1.0 / (1.0 + jnp.exp(-x))


def _rms(x, g):
    return x * lax.rsqrt(jnp.mean(x * x, axis=-1, keepdims=True) + NORM_EPS) * g


def _exp_parts(blocks, extra=None):
    m = None
    for s in blocks:
        mi = jnp.max(s, axis=-1, keepdims=True)
        m = mi if m is None else jnp.maximum(m, mi)
    if extra is not None:
        m = jnp.maximum(m, extra)
    es = [jnp.exp2(s - m) for s in blocks]
    l = None
    for e in es:
        li = jnp.sum(e, axis=-1, keepdims=True)
        l = li if l is None else l + li
    if extra is not None:
        l = l + jnp.exp2(extra - m)
    return es, l


def _lane(shape):
    return lax.broadcasted_iota(jnp.int32, shape, 1)


def _rope(x, cos, sin_signed, half):
    lane = _lane(x.shape)
    partner = jnp.where((lane % (2 * half)) < half,
                        pltpu.roll(x, LANES - half, 1), pltpu.roll(x, half, 1))
    return x * cos + partner * sin_signed


def _lambda(lp, lam_init):
    s1 = jnp.sum(lp[0:1, :] * lp[1:2, :], axis=-1, keepdims=True)
    s2 = jnp.sum(lp[2:3, :] * lp[3:4, :], axis=-1, keepdims=True)
    return jnp.exp(s1) - jnp.exp(s2) + lam_init


def _subln(o, g2, lam_init):
    lane = _lane(o.shape)
    sq = o * o
    ms0 = jnp.sum(jnp.where(lane < HEAD_DIM, sq, 0.0), axis=-1, keepdims=True)
    ms1 = jnp.sum(jnp.where(lane >= HEAD_DIM, sq, 0.0), axis=-1, keepdims=True)
    ms = jnp.where(lane < HEAD_DIM, ms0, ms1) * (1.0 / HEAD_DIM)
    return (o * lax.rsqrt(ms + NORM_EPS) * g2) * (1.0 - lam_init)


def _adaln_kernel(cond_ref, w_ref, b_ref, o_ref):
    c = cond_ref[...]
    s = c * _sigmoid(c)
    s_hi, s_lo = _split(s)
    w_hi, w_lo = _split(w_ref[...])
    o_ref[...] = _dot3(s_hi, s_lo, w_hi, w_lo) + b_ref[...]


def _adaln(cond, ada_w, ada_b):
    tn = 1536
    n = 6 * D_MODEL
    return pl.pallas_call(
        _adaln_kernel,
        grid=(DEPTH, n // tn),
        in_specs=[
            pl.BlockSpec((COND_ROWS, D_MODEL), lambda l, j: (0, 0)),
            pl.BlockSpec((None, D_MODEL, tn), lambda l, j: (l, 0, j)),
            pl.BlockSpec((None, 1, tn), lambda l, j: (l, 0, j)),
        ],
        out_specs=pl.BlockSpec((None, COND_ROWS, tn), lambda l, j: (l, 0, j)),
        out_shape=jax.ShapeDtypeStruct((DEPTH, COND_ROWS, n), F32),
        compiler_params=_params(("arbitrary", "arbitrary")),
        name="adaln",
    )(cond, ada_w, ada_b.reshape(DEPTH, 1, n))


def _cond_row(i):
    n_ctx = N_CTX_TOK // TOKEN_TILE
    per_seq = DEC_SEQ // TOKEN_TILE
    return jnp.where(i < n_ctx, 0, 1 + (i - n_ctx) // per_seq)


def _mod_spec(layer):
    return pl.BlockSpec((None, None, 6, D_MODEL), lambda i: (layer, _cond_row(i), 0, 0))


N_CTX_TILES = N_CTX_TOK // TOKEN_TILE
N_LAT_TILES = N_LAT_TOK // TOKEN_TILE


def _tok_spec(width):
    return pl.BlockSpec((TOKEN_TILE, width), lambda i: (i, 0))


def _ctx_tile_spec(width):
    return pl.BlockSpec((TOKEN_TILE, width), lambda i: (jnp.minimum(i, N_CTX_TILES - 1), 0))


def _lat_tile_spec(width):
    return pl.BlockSpec((TOKEN_TILE, width), lambda i: (jnp.clip(i - N_CTX_TILES, 0, N_LAT_TILES - 1), 0))


def _const_spec(shape):
    nd = len(shape)
    return pl.BlockSpec(shape, lambda i: (0,) * nd, pipeline_mode=pl.Buffered(1))


def _is_ctx_tile():
    return pl.program_id(0) < N_CTX_TILES


def _x_specs(x):
    return [_ctx_tile_spec(D_MODEL), _lat_tile_spec(D_MODEL)] if isinstance(x, tuple) else [_tok_spec(D_MODEL)]


def _x_args(x):
    return list(x) if isinstance(x, tuple) else [x]


def _load_x(x_refs):
    if len(x_refs) == 1:
        return x_refs[0][...]
    return jnp.where(_is_ctx_tile(), x_refs[0][...], x_refs[1][...])


def _proj_kernel(*refs, n_x):
    x_refs = refs[:n_x]
    mod_ref, g_ref, w_ref = refs[n_x:n_x + 3]
    p_ref, na_ref, swa_ref, diff_ref = refs[-4:]
    h = _rms(_load_x(x_refs), g_ref[...]) * (1.0 + mod_ref[1:2, :]) + mod_ref[0:1, :]
    p = _dot(h.astype(BF16), w_ref[...])
    p_ref[...] = p

    @pl.when(_is_ctx_tile())
    def _():
        for s in range(TOKEN_TILE // SEQ):
            rows = slice(s * SEQ, (s + 1) * SEQ)
            for which in range(2):
                na_ref[s, which] = p[rows, NA_K + 256 * which: NA_K + 256 * (which + 1)]
                swa_ref[s, which] = p[rows, SWA_K + LANES * which: SWA_K + LANES * (which + 1)]
                diff_ref[s, which] = p[rows, DIFF_K + 256 * which: DIFF_K + 256 * (which + 1)]


STATE_WIDTHS = (256, LANES, 256)


def _proj(x, mod, layer, g, w, states):
    seqs = TOKEN_TILE // SEQ

    def state_spec(width):
        return pl.BlockSpec((seqs, None, 2, SEQ, width),
                            lambda i: (jnp.minimum(i, N_CTX_TILES - 1), layer, 0, 0, 0))

    n_x = len(_x_args(x))
    n_in = n_x + 3
    prior = [] if states is None else list(states)
    return pl.pallas_call(
        functools.partial(_proj_kernel, n_x=n_x),
        grid=(N_TOK // TOKEN_TILE,),
        in_specs=_x_specs(x) + [_mod_spec(layer), _const_spec((1, D_MODEL)), _const_spec((D_MODEL, D_IN))]
                 + [pl.BlockSpec(memory_space=pl.ANY)] * len(prior),
        out_specs=[_tok_spec(D_IN)] + [state_spec(w_) for w_ in STATE_WIDTHS],
        out_shape=[jax.ShapeDtypeStruct((N_TOK, D_IN), F32)]
                  + [jax.ShapeDtypeStruct((BATCH, DEPTH, 2, SEQ, w_), F32) for w_ in STATE_WIDTHS],
        input_output_aliases={n_in + k: 1 + k for k in range(len(prior))},
        compiler_params=_params(("arbitrary",), VMEM_LIMIT),
        name="proj",
    )(*_x_args(x), mod, g, w, *prior)


N_BRANCH = 4
BRANCH_DIM = D_MODEL // N_BRANCH


def _merge_kernel(*refs, n_x):
    x_refs = refs[:n_x]
    mod_ref, g_ref = refs[n_x:n_x + 2]
    ctx_refs = refs[n_x + 2:n_x + 2 + N_BRANCH]
    lat_refs = refs[n_x + 2 + N_BRANCH:n_x + 2 + 2 * N_BRANCH]
    wg_ref, bg_ref, wb_ref, wo_ref, o_ref = refs[n_x + 2 + 2 * N_BRANCH:]
    is_ctx = _is_ctx_tile()
    x = _load_x(x_refs)
    h = _rms(x, g_ref[...]) * (1.0 + mod_ref[1:2, :]) + mod_ref[0:1, :]
    hb = h.astype(BF16)
    merged = None
    for k in range(N_BRANCH):
        cols = slice(k * D_MODEL, (k + 1) * D_MODEL)
        gate = _sigmoid(_dot(hb, wg_ref[:, cols]) + bg_ref[:, cols])
        branch = jnp.where(is_ctx, ctx_refs[k][...], lat_refs[k][...])
        term = gate * _dot(branch, wb_ref[k])
        merged = term if merged is None else merged + term
    o_ref[...] = x + mod_ref[2:3, :] * _dot(merged.astype(BF16), wo_ref[...])


def _merge(x, mod, layer, g, ctx_branches, lat_branches, wg, bg, wb, wo):
    return pl.pallas_call(
        functools.partial(_merge_kernel, n_x=len(_x_args(x))),
        grid=(N_TOK // TOKEN_TILE,),
        in_specs=_x_specs(x) + [_mod_spec(layer), _const_spec((1, D_MODEL))]
                 + [_ctx_tile_spec(BRANCH_DIM)] * N_BRANCH + [_lat_tile_spec(BRANCH_DIM)] * N_BRANCH
                 + [_const_spec((D_MODEL, N_BRANCH * D_MODEL)), _const_spec((1, N_BRANCH * D_MODEL)),
                    _const_spec((N_BRANCH, BRANCH_DIM, D_MODEL)), _const_spec((D_MODEL, D_MODEL))],
        out_specs=_tok_spec(D_MODEL),
        out_shape=jax.ShapeDtypeStruct((N_TOK, D_MODEL), F32),
        compiler_params=_params(("arbitrary",), VMEM_LIMIT),
        name="merge",
    )(*_x_args(x), mod, g, *ctx_branches, *lat_branches, wg, bg, wb, wo)


def _ffn_kernel(x_ref, mod_ref, g_ref, wi_ref, wo_ref, fg_ref, *o_refs, final):
    x = x_ref[...]
    h = _rms(x, g_ref[...]) * (1.0 + mod_ref[4:5, :]) + mod_ref[3:4, :]
    hb = h.astype(BF16)
    a = _dot(hb, wi_ref[:, :D_FF])
    b = _dot(hb, wi_ref[:, D_FF:])
    f = (a * _sigmoid(a)) * b
    y = x + mod_ref[5:6, :] * _dot(f.astype(BF16), wo_ref[...])
    if not final:
        o_refs[0][...] = y
        return
    y = _rms(y, fg_ref[...])
    ctx_ref, lat_ref = o_refs

    @pl.when(_is_ctx_tile())
    def _():
        ctx_ref[...] = y

    @pl.when(jnp.logical_not(_is_ctx_tile()))
    def _():
        lat_ref[...] = y


def _ffn(x, mod, layer, g, wi, wo, fg, final):
    if final:
        out_specs = [_ctx_tile_spec(D_MODEL), _lat_tile_spec(D_MODEL)]
        out_shape = [jax.ShapeDtypeStruct((N_CTX_TOK, D_MODEL), F32),
                     jax.ShapeDtypeStruct((N_LAT_TOK, D_MODEL), F32)]
    else:
        out_specs = _tok_spec(D_MODEL)
        out_shape = jax.ShapeDtypeStruct((N_TOK, D_MODEL), F32)
    return pl.pallas_call(
        functools.partial(_ffn_kernel, final=final),
        grid=(N_TOK // TOKEN_TILE,),
        in_specs=[_tok_spec(D_MODEL), _mod_spec(layer), _const_spec((1, D_MODEL)),
                  _const_spec((D_MODEL, 2 * D_FF)), _const_spec((D_FF, D_MODEL)),
                  _const_spec((1, D_MODEL))],
        out_specs=out_specs,
        out_shape=out_shape,
        compiler_params=_params(("arbitrary",), VMEM_LIMIT),
        name="ffn",
    )(x, mod, g, wi, wo, fg)


@functools.lru_cache(maxsize=None)
def _dft_tables(n, blocks):
    k = np.arange(n, dtype=np.int64)
    ang = 2.0 * np.pi * ((k[:, None] * k[None, :]) % n).astype(np.float64) / n
    out = []
    for m in (np.cos(ang), np.sin(ang)):
        m = np.kron(np.eye(blocks), m / math.sqrt(n)).astype(np.float32)
        hi = m.astype(BF16)
        lo = (m - hi.astype(np.float32)).astype(BF16)
        out += [hi, lo]
    return tuple(out)


def _fourier_kernel(u_ref, ch_ref, cl_ref, sh_ref, sl_ref, cch_ref, ccl_ref, sch_ref, scl_ref, o_ref):
    u_hi, u_lo = _split(u_ref[...])
    a_hi, a_lo = _split(_dot3(ch_ref[...], cl_ref[...], u_hi, u_lo))
    b_hi, b_lo = _split(_dot3(sh_ref[...], sl_ref[...], u_hi, u_lo))
    y = _dot3(a_hi, a_lo, cch_ref[...], ccl_ref[...]) - _dot3(b_hi, b_lo, sch_ref[...], scl_ref[...])
    o_ref[...] = y.astype(o_ref.dtype)


def _fourier(p, seq, n_seq, first_tok):
    tf = 256
    nf = seq // tf
    width = 256
    first_blk = first_tok // seq
    pos = [jnp.asarray(t) for t in _dft_tables(seq, 1)]
    chan = [jnp.asarray(t) for t in _dft_tables(width // 4, 4)]
    pos_spec = pl.BlockSpec((tf, seq), lambda f, b: (f, 0))
    chan_spec = pl.BlockSpec((width, width), lambda f, b: (0, 0))
    return pl.pallas_call(
        _fourier_kernel,
        grid=(nf, n_seq),
        in_specs=[pl.BlockSpec((seq, width), lambda f, b: (first_blk + b, FNET_U // width))]
                 + [pos_spec] * 4 + [chan_spec] * 4,
        out_specs=pl.BlockSpec((tf, width), lambda f, b: (b * nf + f, 0)),
        out_shape=jax.ShapeDtypeStruct((n_seq * seq, width), BF16),
        compiler_params=_params(("arbitrary", "arbitrary"), VMEM_LIMIT),
        name="fourier",
    )(p, *pos, *chan)


def _ctx_mixer_kernel(p_ref, sink_ref, lp_ref, sg_ref, na_ref, swa_ref, diff_ref, *, lam_init):
    shape = (SEQ, LANES)
    lane = _lane(shape)
    half = lane // HEAD_DIM
    quarter = lane // DIFF_QK_DIM
    scale = HEAD_DIM ** -0.5 * LOG2E

    def pair(col, j):
        return p_ref[:, col + LANES * j: col + LANES * (j + 1)]

    for j in range(2):
        q2 = pair(NA_Q, j) * scale
        kb = pair(NA_K, j).astype(BF16)
        vb = pair(NA_V, j).astype(BF16)
        outs = []
        for g in range(2):
            qm = jnp.where(half == g, q2, 0.0).astype(BF16)
            (e,), l = _exp_parts([_dot_nt(qm, kb)])
            outs.append(_dot(e.astype(BF16), vb) / l)
        na_ref[:, LANES * j: LANES * (j + 1)] = jnp.where(half == 0, outs[0], outs[1]).astype(na_ref.dtype)

    k2 = p_ref[:, SWA_K: SWA_K + LANES]
    v2 = p_ref[:, SWA_V: SWA_V + LANES]
    k2s = pltpu.roll(k2, HEAD_DIM, 1)
    v2s = pltpu.roll(v2, HEAD_DIM, 1)
    for j in range(2):
        kb = jnp.where(half == j, k2, k2s).astype(BF16)
        vb = jnp.where(half == j, v2, v2s).astype(BF16)
        q2 = pair(SWA_Q, j) * scale
        outs = []
        for g in range(2):
            qm = jnp.where(half == g, q2, 0.0).astype(BF16)
            (e,), l = _exp_parts([_dot_nt(qm, kb)], extra=sink_ref[2 * j + g] * LOG2E)
            outs.append(_dot(e.astype(BF16), vb) / l)
        swa_ref[:, LANES * j: LANES * (j + 1)] = jnp.where(half == 0, outs[0], outs[1]).astype(swa_ref.dtype)

    lam = _lambda(lp_ref[...], lam_init)
    for j in range(2):
        q2 = pair(DIFF_Q, j) * (DIFF_QK_DIM ** -0.5 * LOG2E)
        kb = pair(DIFF_K, j).astype(BF16)
        vb = pair(DIFF_V, j).astype(BF16)
        outs = []
        for g in range(2):
            o = []
            for c in range(2):
                qm = jnp.where(quarter == 2 * g + c, q2, 0.0).astype(BF16)
                (e,), l = _exp_parts([_dot_nt(qm, kb)])
                o.append(_dot(e.astype(BF16), vb) / l)
            outs.append(o[0] - lam * o[1])
        o2 = jnp.where(half == 0, outs[0], outs[1])
        diff_ref[:, LANES * j: LANES * (j + 1)] = _subln(o2, sg_ref[...], lam_init).astype(diff_ref.dtype)


def _ctx_mixers(p, sink, lp, sg2, lam_init):
    out = jax.ShapeDtypeStruct((N_CTX_TOK, 256), BF16)
    ospec = pl.BlockSpec((SEQ, 256), lambda b: (b, 0))
    return pl.pallas_call(
        functools.partial(_ctx_mixer_kernel, lam_init=lam_init),
        grid=(BATCH,),
        in_specs=[pl.BlockSpec((SEQ, D_IN), lambda b: (b, 0)),
                  pl.BlockSpec(memory_space=pltpu.SMEM),
                  pl.BlockSpec((4, DIFF_QK_DIM), lambda b: (0, 0)),
                  pl.BlockSpec((1, LANES), lambda b: (0, 0))],
        out_specs=[ospec, ospec, ospec],
        out_shape=[out, out, out],
        compiler_params=_params(("arbitrary",), VMEM_LIMIT),
        name="ctx_mixers",
    )(p, sink, lp, sg2)


N_DR = 2 * NA_WIN_H - 1
N_DC = 2 * NA_WIN_W - 1
N_PAIR_BLOCKS = N_DR + 1


def _na_band_start(t):
    return jnp.clip(NA_TILE_ROWS * t - NA_WIN_H // 2, 0, GRID_H - NA_BAND_ROWS)


def _na_bias_rows(rpb):
    rows = jnp.pad(rpb, ((0, 0), (1, 1), (0, GRID_W - N_DC)))
    return jnp.concatenate([rows[:, :N_PAIR_BLOCKS], rows[:, 1:]], axis=-1)


def _lat_na_kernel(q_ref, k_ref, v_ref, ck_ref, cv_ref, rows_ref, o_ref, pair_ref, keys_ref, vals_ref):
    t = pl.program_id(2)
    blk = (GRID_W, LANES)

    @pl.when(t == 0)
    def _():
        c = lax.broadcasted_iota(jnp.int32, blk, 0)
        kc = _lane(blk) % GRID_W
        cs = jnp.clip(c - NA_WIN_W // 2, 0, GRID_W - NA_WIN_W)
        in_cols = (kc >= cs) & (kc < cs + NA_WIN_W)
        for g in range(2):
            for i in range(N_PAIR_BLOCKS):
                row = jnp.broadcast_to(rows_ref[g, i:i + 1, :], blk) * LOG2E
                toeplitz = pltpu.roll(row, LANES - (NA_WIN_W - 1), 1, stride=1, stride_axis=0)
                pair_ref[g, i] = jnp.where(in_cols, toeplitz, NEG_INF)

        _stage_keys_values(keys_ref, vals_ref, k_ref[...], v_ref[...], ck_ref[...], cv_ref[...])

    band = _na_band_start(t)
    start = pl.multiple_of(band * GRID_W, GRID_W)
    nk = NA_BAND_ROWS * GRID_W
    q2 = q_ref[...] * (HEAD_DIM ** -0.5 * LOG2E)

    def bias(g):
        rows = []
        for ri in range(NA_TILE_ROWS):
            r = NA_TILE_ROWS * t + ri
            first = jnp.clip(r - NA_WIN_H // 2, 0, GRID_H - NA_WIN_H) - band
            d0 = band - r + NA_WIN_H - 1
            blocks = []
            for m in range(NA_BAND_ROWS // 2):
                idx = jnp.clip(d0 + 2 * m + 1, 0, N_PAIR_BLOCKS - 1)
                jrow = 2 * m + _lane(blk) // GRID_W
                in_rows = (jrow >= first) & (jrow < first + NA_WIN_H)
                blocks.append(jnp.where(in_rows, pair_ref[g, idx], NEG_INF))
            rows.append(jnp.concatenate(blocks, axis=1))
        return jnp.concatenate(rows, axis=0)

    o_ref[...] = _attend_pair(q2, keys_ref, vals_ref, start, nk, lambda g, s: s + bias(g)).astype(o_ref.dtype)


def _stage_keys_values(keys_ref, vals_ref, k2, v2, ck2, cv2):
    keys_ref[:DEC_SEQ, :] = k2.astype(BF16)
    keys_ref[DEC_SEQ:, :] = ck2.astype(BF16)
    for g in range(2):
        vals_ref[g, :DEC_SEQ, :] = jnp.where(_lane(v2.shape) // HEAD_DIM == g, v2, 1.0).astype(BF16)
        vals_ref[g, DEC_SEQ:, :] = jnp.where(_lane(cv2.shape) // HEAD_DIM == g, cv2, 1.0).astype(BF16)


def _attend_pair(q2, keys_ref, vals_ref, start, n_loc, fix_local, sink=None):
    half = _lane(q2.shape) // HEAD_DIM
    loc = pl.ds(start, n_loc)
    ctx = slice(DEC_SEQ, DEC_SEQ + PAST_LEN)
    scores = []
    for g in range(2):
        qm = jnp.where(half == g, q2, 0.0).astype(BF16)
        scores.append((fix_local(g, _dot_nt(qm, keys_ref[loc, :])), _dot_nt(qm, keys_ref[ctx, :])))
    outs = []
    for g in range(2):
        s_loc, s_ctx = scores[g]
        m = jnp.maximum(jnp.max(s_loc, axis=-1, keepdims=True), jnp.max(s_ctx, axis=-1, keepdims=True))
        if sink is not None:
            m = jnp.maximum(m, sink[g])
        acc = (_dot(jnp.exp2(s_loc - m).astype(BF16), vals_ref[g, loc, :])
               + _dot(jnp.exp2(s_ctx - m).astype(BF16), vals_ref[g, ctx, :]))
        if sink is not None:
            acc = acc + jnp.where(half == g, 0.0, jnp.exp2(sink[g] - m))
        outs.append(acc / jnp.where(half == g, pltpu.roll(acc, HEAD_DIM, 1), 1.0))
    return jnp.where(half == 0, outs[0], outs[1])


def _staging_scratch():
    n = DEC_SEQ + PAST_LEN
    return [pltpu.VMEM((n, LANES), BF16), pltpu.VMEM((2, n, LANES), BF16)]


def _lat_blocks(col):
    qt = DEC_SEQ // Q_TILE
    first_q = N_CTX_TOK // Q_TILE
    first_k = N_CTX_TOK // DEC_SEQ
    q = pl.BlockSpec((Q_TILE, LANES), lambda b, j, t: (first_q + qt * b + t, col[0] // LANES + j))
    k = pl.BlockSpec((DEC_SEQ, LANES), lambda b, j, t: (first_k + b, col[1] // LANES + j))
    v = pl.BlockSpec((DEC_SEQ, LANES), lambda b, j, t: (first_k + b, col[2] // LANES + j))
    o = pl.BlockSpec((Q_TILE, LANES), lambda b, j, t: (qt * b + t, j))
    return q, k, v, o


def _cache_spec(layer, which, shared_kv):
    return pl.BlockSpec((None, None, None, PAST_LEN, LANES),
                        lambda b, j, t: (b, layer, which, 0, 0 if shared_kv else j))


def _lat_na(p, cache, bias_rows, layer):
    q, k, v, o = _lat_blocks((NA_Q, NA_K, NA_V))
    rows_spec = pl.BlockSpec((2, N_PAIR_BLOCKS, LANES), lambda b, j, t: (j, 0, 0))
    return pl.pallas_call(
        _lat_na_kernel,
        grid=(DEC_BATCH, 2, DEC_SEQ // Q_TILE),
        in_specs=[q, k, v, _cache_spec(layer, 0, False), _cache_spec(layer, 1, False), rows_spec],
        out_specs=o,
        out_shape=jax.ShapeDtypeStruct((N_LAT_TOK, 256), BF16),
        scratch_shapes=[pltpu.VMEM((2, N_PAIR_BLOCKS, GRID_W, LANES), F32)] + _staging_scratch(),
        compiler_params=_params(("arbitrary",) * 3, VMEM_LIMIT),
        name="lat_na",
    )(p, p, p, cache, cache, bias_rows)


@functools.lru_cache(maxsize=None)
def _rope_tables(dim):
    quarter = dim // 4
    pos = np.arange(DEC_SEQ)
    rows, cols = pos // GRID_W, pos % GRID_W
    lane = np.arange(LANES)
    w = lane % dim
    axis_pos = np.where((w // (dim // 2) == 0)[None, :], rows[:, None], cols[:, None]).astype(np.float64)
    u = w % (dim // 2)
    inv = ROPE_BASE ** (-(u % quarter).astype(np.float64) * 2.0 / (dim // 2))
    ang = axis_pos * inv[None, :]
    sign = np.where(u < quarter, -1.0, 1.0)[None, :]
    return np.cos(ang).astype(np.float32), (np.sin(ang) * sign).astype(np.float32)


def _swa_key_start(t):
    return jnp.clip(Q_TILE * t - SWA_WINDOW, 0, DEC_SEQ - SWA_KEYS)


def _lat_swa_kernel(q_ref, k_ref, v_ref, ck_ref, cv_ref, cos_ref, sin_ref, sink_ref, o_ref, keys_ref, vals_ref):
    j = pl.program_id(1)
    t = pl.program_id(2)
    quarter = HEAD_DIM // 4

    @pl.when(t == 0)
    def _():
        def head_j(x):
            return jnp.where(_lane(x.shape) // HEAD_DIM == j, x, pltpu.roll(x, HEAD_DIM, 1))

        _stage_keys_values(keys_ref, vals_ref, head_j(_rope(k_ref[...], cos_ref[...], sin_ref[...], quarter)),
                           head_j(v_ref[...]), head_j(ck_ref[...]), head_j(cv_ref[...]))

    q0 = pl.multiple_of(t * Q_TILE, Q_TILE)
    k0 = pl.multiple_of(_swa_key_start(t), SWA_WINDOW)
    q2 = _rope(q_ref[...], cos_ref[pl.ds(q0, Q_TILE), :], sin_ref[pl.ds(q0, Q_TILE), :], quarter)
    q2 = q2 * (HEAD_DIM ** -0.5 * LOG2E)
    qpos = q0 + lax.broadcasted_iota(jnp.int32, (Q_TILE, SWA_KEYS), 0)
    kpos = k0 + lax.broadcasted_iota(jnp.int32, (Q_TILE, SWA_KEYS), 1)
    valid = jnp.abs(kpos - qpos) <= SWA_WINDOW
    sink = [sink_ref[2 * j + g] * LOG2E for g in range(2)]
    out = _attend_pair(q2, keys_ref, vals_ref, k0, SWA_KEYS, lambda g, s: jnp.where(valid, s, NEG_INF), sink)
    o_ref[...] = out.astype(o_ref.dtype)


def _lat_swa(p, cache, sink, layer):
    q, _, _, o = _lat_blocks((SWA_Q, SWA_K, SWA_V))
    first_k = N_CTX_TOK // DEC_SEQ
    k = pl.BlockSpec((DEC_SEQ, LANES), lambda b, j, t: (first_k + b, SWA_K // LANES))
    v = pl.BlockSpec((DEC_SEQ, LANES), lambda b, j, t: (first_k + b, SWA_V // LANES))
    cos, sin = (jnp.asarray(a) for a in _rope_tables(HEAD_DIM))
    tab = pl.BlockSpec((DEC_SEQ, LANES), lambda b, j, t: (0, 0))
    return pl.pallas_call(
        _lat_swa_kernel,
        grid=(DEC_BATCH, 2, DEC_SEQ // Q_TILE),
        in_specs=[q, k, v, _cache_spec(layer, 0, True), _cache_spec(layer, 1, True), tab, tab,
                  pl.BlockSpec(memory_space=pltpu.SMEM)],
        out_specs=o,
        out_shape=jax.ShapeDtypeStruct((N_LAT_TOK, 256), BF16),
        scratch_shapes=_staging_scratch(),
        compiler_params=_params(("arbitrary",) * 3, VMEM_LIMIT),
        name="lat_swa",
    )(p, p, p, cache, cache, cos, sin, sink)


DIFF_KEYS = DEC_SEQ + PAST_LEN
KEY_CHUNK = 256


def _lat_diff_kernel(q_ref, k_ref, v_ref, ck_ref, cv_ref, cos_ref, sin_ref, lp_ref, sg_ref, o_ref,
                     keys_ref, vals_ref, s_ref, *, lam_init):
    t = pl.program_id(2)
    eighth = DIFF_QK_DIM // 4

    @pl.when(t == 0)
    def _():
        _stage_keys_values(keys_ref, vals_ref, _rope(k_ref[...], cos_ref[...], sin_ref[...], eighth), v_ref[...],
                           ck_ref[...], cv_ref[...])

    q0 = pl.multiple_of(t * Q_TILE, Q_TILE)
    q2 = _rope(q_ref[...], cos_ref[pl.ds(q0, Q_TILE), :], sin_ref[pl.ds(q0, Q_TILE), :], eighth)
    q2 = q2 * (DIFF_QK_DIM ** -0.5 * LOG2E)
    lam = _lambda(lp_ref[...], lam_init)
    lane = _lane(q2.shape)
    quarter = lane // DIFF_QK_DIM
    half = lane // HEAD_DIM
    n_chunks = DIFF_KEYS // KEY_CHUNK

    def scores(i):
        qm = jnp.where(quarter == i, q2, 0.0).astype(BF16)
        m_run = None
        for n in range(n_chunks):
            cols = slice(n * KEY_CHUNK, (n + 1) * KEY_CHUNK)
            s = _dot_nt(qm, keys_ref[cols, :])
            s_ref[i, :, cols] = s
            m_blk = jnp.maximum(s[:, :LANES], s[:, LANES:])
            m_run = m_blk if m_run is None else jnp.maximum(m_run, m_blk)
        return jnp.max(m_run, axis=-1, keepdims=True)

    def values(i, m):
        g = i // 2
        acc = None
        for n in range(n_chunks):
            cols = slice(n * KEY_CHUNK, (n + 1) * KEY_CHUNK)
            part = _dot(jnp.exp2(s_ref[i, :, cols] - m).astype(BF16), vals_ref[g, cols, :])
            acc = part if acc is None else acc + part
        return acc / jnp.where(half == g, pltpu.roll(acc, HEAD_DIM, 1), 1.0)

    o = []
    m_next = scores(0)
    for i in range(4):
        m = m_next
        if i + 1 < 4:
            m_next = scores(i + 1)
        o.append(values(i, m))
    outs = [o[0] - lam * o[1], o[2] - lam * o[3]]
    o2 = jnp.where(half == 0, outs[0], outs[1])
    o_ref[...] = _subln(o2, sg_ref[...], lam_init).astype(o_ref.dtype)


def _lat_diff(p, cache, lp, sg2, layer, lam_init):
    q, k, v, o = _lat_blocks((DIFF_Q, DIFF_K, DIFF_V))
    cos, sin = (jnp.asarray(a) for a in _rope_tables(DIFF_QK_DIM))
    tab = pl.BlockSpec((DEC_SEQ, LANES), lambda b, j, t: (0, 0))
    return pl.pallas_call(
        functools.partial(_lat_diff_kernel, lam_init=lam_init),
        grid=(DEC_BATCH, 2, DEC_SEQ // Q_TILE),
        in_specs=[q, k, v, _cache_spec(layer, 0, False), _cache_spec(layer, 1, False), tab, tab,
                  pl.BlockSpec((4, DIFF_QK_DIM), lambda b, j, t: (0, 0)),
                  pl.BlockSpec((1, LANES), lambda b, j, t: (0, 0))],
        out_specs=o,
        out_shape=jax.ShapeDtypeStruct((N_LAT_TOK, 256), BF16),
        scratch_shapes=_staging_scratch() + [pltpu.VMEM((4, Q_TILE, DIFF_KEYS), F32)],
        compiler_params=_params(("arbitrary",) * 3, VMEM_LIMIT),
        name="lat_diff",
    )(p, p, p, cache, cache, cos, sin, lp, sg2)


def kernel(x_prompt, x_sample, cache_na_kv, cache_swa_kv, cache_diff_kv, c, c_ctx, norm1_g, norm2_g, ada_w,
           ada_b, w_in, na_rpb, swa_sink, diff_lambda, diff_subln_g, w_branch, w_gate, b_gate, w_o, w_ffn_in,
           w_ffn_out, final_norm_g):
    x = (x_prompt.reshape(N_CTX_TOK, D_MODEL), x_sample.reshape(N_LAT_TOK, D_MODEL))
    cond = jnp.zeros((COND_ROWS, D_MODEL), F32).at[0].set(c_ctx).at[1:1 + DEC_BATCH].set(c)
    mod = _adaln(cond, ada_w, ada_b).reshape(DEPTH, COND_ROWS, 6, D_MODEL)

    cache_na = cache_na_kv.reshape(DEC_BATCH, DEPTH, 2, PAST_LEN, 256)
    cache_swa = cache_swa_kv.reshape(DEC_BATCH, DEPTH, 2, PAST_LEN, LANES)
    cache_diff = cache_diff_kv.reshape(DEC_BATCH, DEPTH, 2, PAST_LEN, 256)
    fg = final_norm_g.reshape(1, D_MODEL)

    states = None
    for l in range(DEPTH):
        lam_init = 0.8 - 0.6 * math.exp(-0.3 * l)
        g1 = norm1_g[l].reshape(1, D_MODEL)
        sg2 = jnp.tile(diff_subln_g[l], 2).reshape(1, LANES)
        p, *states = _proj(x, mod, l, g1, w_in[l].astype(BF16), states)

        c_na, c_swa, c_diff = _ctx_mixers(p, swa_sink[l], diff_lambda[l], sg2, lam_init)
        c_f = _fourier(p, SEQ, BATCH, 0)
        l_na = _lat_na(p, cache_na, _na_bias_rows(na_rpb[l]), l)
        l_swa = _lat_swa(p, cache_swa, swa_sink[l], l)
        l_f = _fourier(p, DEC_SEQ, DEC_BATCH, N_CTX_TOK)
        l_diff = _lat_diff(p, cache_diff, diff_lambda[l], sg2, l, lam_init)

        x = _merge(x, mod, l, g1, (c_na, c_swa, c_f, c_diff), (l_na, l_swa, l_f, l_diff),
                   w_gate[l].astype(BF16), b_gate[l].reshape(1, -1), w_branch[l].astype(BF16),
                   w_o[l].astype(BF16))
        x = _ffn(x, mod, l, norm2_g[l].reshape(1, D_MODEL), w_ffn_in[l].astype(BF16),
                 w_ffn_out[l].astype(BF16), fg, final=(l == DEPTH - 1))

    y_ctx, y_lat = x
    na_kv, swa_kv, diff_kv = states
    return (y_ctx.reshape(BATCH, SEQ, D_MODEL), y_lat.reshape(DEC_BATCH, DEC_SEQ, D_MODEL),
            na_kv.reshape(BATCH, DEPTH, 2, SEQ, 4, HEAD_DIM), swa_kv.reshape(BATCH, DEPTH, 2, SEQ, 2, HEAD_DIM),
            diff_kv.reshape(BATCH, DEPTH, 2, SEQ, 4, HEAD_DIM))
```

```python
import functools
import math

import numpy as np
import jax
import jax.numpy as jnp
from jax import lax
from jax.experimental import pallas as pl
from jax.experimental.pallas import tpu as pltpu

D_MODEL = 1024
BATCH = 16
SEQ = 256
DEPTH = 2
DEC_BATCH = 4
DEC_SEQ = 2048
PAST_LEN = 512
GRID_W = 64
GRID_H = DEC_SEQ // GRID_W
HEAD_DIM = 64
NA_WIN_H = 8
NA_WIN_W = 16
SWA_WINDOW = 128
DIFF_QK_DIM = 32
D_FF = 2816
D_IN = 2304
ROPE_BASE = 10000.0
NORM_EPS = 1e-6
NEG_INF = -1e30
LOG2E = math.log2(math.e)

NA_Q, NA_K, NA_V = 0, 256, 512
SWA_Q, SWA_K, SWA_V = 768, 1024, 1152
FNET_U = 1280
DIFF_Q, DIFF_K, DIFF_V = 1536, 1792, 2048

LANES = 128
N_CTX_TOK = BATCH * SEQ
N_LAT_TOK = DEC_BATCH * DEC_SEQ
N_TOK = N_CTX_TOK + N_LAT_TOK
COND_ROWS = 8
TOKEN_TILE = 512
Q_TILE = 256
NA_TILE_ROWS = Q_TILE // GRID_W
NA_BAND_ROWS = NA_WIN_H + NA_TILE_ROWS
SWA_KEYS = 2 * Q_TILE
VMEM_LIMIT = 56 * 1024 * 1024

F32 = jnp.float32
BF16 = jnp.bfloat16


def _params(semantics, vmem=None):
    return pltpu.CompilerParams(dimension_semantics=semantics, vmem_limit_bytes=vmem)


def _dot(a, b):
    return jnp.dot(a, b, preferred_element_type=F32)


def _dot_nt(a, b):
    return lax.dot_general(a, b, (((1,), (1,)), ((), ())), preferred_element_type=F32)


def _split(x):
    hi = x.astype(BF16)
    lo = (x - hi.astype(F32)).astype(BF16)
    return hi, lo


def _dot3(a_hi, a_lo, b_hi, b_lo):
    return _dot(a_hi, b_hi) + _dot(a_lo, b_hi) + _dot(a_hi, b_lo)


def _sigmoid(x):
    return 1.0 / (1.0 + jnp.exp(-x))


def _rms(x, g):
    return x * lax.rsqrt(jnp.mean(x * x, axis=-1, keepdims=True) + NORM_EPS) * g


def _exp_parts(blocks, extra=None):
    m = None
    for s in blocks:
        mi = jnp.max(s, axis=-1, keepdims=True)
        m = mi if m is None else jnp.maximum(m, mi)
    if extra is not None:
        m = jnp.maximum(m, extra)
    es = [jnp.exp2(s - m) for s in blocks]
    l = None
    for e in es:
        li = jnp.sum(e, axis=-1, keepdims=True)
        l = li if l is None else l + li
    if extra is not None:
        l = l + jnp.exp2(extra - m)
    return es, l


def _lane(shape):
    return lax.broadcasted_iota(jnp.int32, shape, 1)


def _rope(x, cos, sin_signed, half):
    lane = _lane(x.shape)
    partner = jnp.where((lane % (2 * half)) < half,
                        pltpu.roll(x, LANES - half, 1), pltpu.roll(x, half, 1))
    return x * cos + partner * sin_signed


def _lambda(lp, lam_init):
    s1 = jnp.sum(lp[0:1, :] * lp[1:2, :], axis=-1, keepdims=True)
    s2 = jnp.sum(lp[2:3, :] * lp[3:4, :], axis=-1, keepdims=True)
    return jnp.exp(s1) - jnp.exp(s2) + lam_init


def _subln(o, g2, lam_init):
    lane = _lane(o.shape)
    sq = o * o
    ms0 = jnp.sum(jnp.where(lane < HEAD_DIM, sq, 0.0), axis=-1, keepdims=True)
    ms1 = jnp.sum(jnp.where(lane >= HEAD_DIM, sq, 0.0), axis=-1, keepdims=True)
    ms = jnp.where(lane < HEAD_DIM, ms0, ms1) * (1.0 / HEAD_DIM)
    return (o * lax.rsqrt(ms + NORM_EPS) * g2) * (1.0 - lam_init)


def _adaln_kernel(cond_ref, w_ref, b_ref, o_ref):
    c = cond_ref[...]
    s = c * _sigmoid(c)
    s_hi, s_lo = _split(s)
    w_hi, w_lo = _split(w_ref[...])
    o_ref[...] = _dot3(s_hi, s_lo, w_hi, w_lo) + b_ref[...]


def _adaln(cond, ada_w, ada_b):
    tn = 1536
    n = 6 * D_MODEL
    return pl.pallas_call(
        _adaln_kernel,
        grid=(DEPTH, n // tn),
        in_specs=[
            pl.BlockSpec((COND_ROWS, D_MODEL), lambda l, j: (0, 0)),
            pl.BlockSpec((None, D_MODEL, tn), lambda l, j: (l, 0, j)),
            pl.BlockSpec((None, 1, tn), lambda l, j: (l, 0, j)),
        ],
        out_specs=pl.BlockSpec((None, COND_ROWS, tn), lambda l, j: (l, 0, j)),
        out_shape=jax.ShapeDtypeStruct((DEPTH, COND_ROWS, n), F32),
        compiler_params=_params(("arbitrary", "arbitrary")),
        name="adaln",
    )(cond, ada_w, ada_b.reshape(DEPTH, 1, n))


def _cond_row(i):
    n_ctx = N_CTX_TOK // TOKEN_TILE
    per_seq = DEC_SEQ // TOKEN_TILE
    return jnp.where(i < n_ctx, 0, 1 + (i - n_ctx) // per_seq)


def _mod_spec(layer):
    return pl.BlockSpec((None, None, 6, D_MODEL), lambda i: (layer, _cond_row(i), 0, 0))


N_CTX_TILES = N_CTX_TOK // TOKEN_TILE
N_LAT_TILES = N_LAT_TOK // TOKEN_TILE


def _tok_spec(width):
    return pl.BlockSpec((TOKEN_TILE, width), lambda i: (i, 0))


def _ctx_tile_spec(width):
    return pl.BlockSpec((TOKEN_TILE, width), lambda i: (jnp.minimum(i, N_CTX_TILES - 1), 0))


def _lat_tile_spec(width):
    return pl.BlockSpec((TOKEN_TILE, width), lambda i: (jnp.clip(i - N_CTX_TILES, 0, N_LAT_TILES - 1), 0))


def _const_spec(shape):
    nd = len(shape)
    return pl.BlockSpec(shape, lambda i: (0,) * nd, pipeline_mode=pl.Buffered(1))


def _is_ctx_tile():
    return pl.program_id(0) < N_CTX_TILES


def _x_specs(x):
    return [_ctx_tile_spec(D_MODEL), _lat_tile_spec(D_MODEL)] if isinstance(x, tuple) else [_tok_spec(D_MODEL)]


def _x_args(x):
    return list(x) if isinstance(x, tuple) else [x]


def _load_x(x_refs):
    if len(x_refs) == 1:
        return x_refs[0][...]
    return jnp.where(_is_ctx_tile(), x_refs[0][...], x_refs[1][...])


def _proj_kernel(*refs, n_x):
    x_refs = refs[:n_x]
    mod_ref, g_ref, w_ref = refs[n_x:n_x + 3]
    p_ref, na_ref, swa_ref, diff_ref = refs[-4:]
    h = _rms(_load_x(x_refs), g_ref[...]) * (1.0 + mod_ref[1:2, :]) + mod_ref[0:1, :]
    p = _dot(h.astype(BF16), w_ref[...])
    p_ref[...] = p

    @pl.when(_is_ctx_tile())
    def _():
        for s in range(TOKEN_TILE // SEQ):
            rows = slice(s * SEQ, (s + 1) * SEQ)
            for which in range(2):
                na_ref[s, which] = p[rows, NA_K + 256 * which: NA_K + 256 * (which + 1)]
                swa_ref[s, which] = p[rows, SWA_K + LANES * which: SWA_K + LANES * (which + 1)]
                diff_ref[s, which] = p[rows, DIFF_K + 256 * which: DIFF_K + 256 * (which + 1)]


STATE_WIDTHS = (256, LANES, 256)


def _proj(x, mod, layer, g, w, states):
    seqs = TOKEN_TILE // SEQ

    def state_spec(width):
        return pl.BlockSpec((seqs, None, 2, SEQ, width),
                            lambda i: (jnp.minimum(i, N_CTX_TILES - 1), layer, 0, 0, 0))

    n_x = len(_x_args(x))
    n_in = n_x + 3
    prior = [] if states is None else list(states)
    return pl.pallas_call(
        functools.partial(_proj_kernel, n_x=n_x),
        grid=(N_TOK // TOKEN_TILE,),
        in_specs=_x_specs(x) + [_mod_spec(layer), _const_spec((1, D_MODEL)), _const_spec((D_MODEL, D_IN))]
                 + [pl.BlockSpec(memory_space=pl.ANY)] * len(prior),
        out_specs=[_tok_spec(D_IN)] + [state_spec(w_) for w_ in STATE_WIDTHS],
        out_shape=[jax.ShapeDtypeStruct((N_TOK, D_IN), F32)]
                  + [jax.ShapeDtypeStruct((BATCH, DEPTH, 2, SEQ, w_), F32) for w_ in STATE_WIDTHS],
        input_output_aliases={n_in + k: 1 + k for k in range(len(prior))},
        compiler_params=_params(("arbitrary",), VMEM_LIMIT),
        name="proj",
    )(*_x_args(x), mod, g, w, *prior)


N_BRANCH = 4
BRANCH_DIM = D_MODEL // N_BRANCH


def _merge_kernel(*refs, n_x):
    x_refs = refs[:n_x]
    mod_ref, g_ref = refs[n_x:n_x + 2]
    ctx_refs = refs[n_x + 2:n_x + 2 + N_BRANCH]
    lat_refs = refs[n_x + 2 + N_BRANCH:n_x + 2 + 2 * N_BRANCH]
    wg_ref, bg_ref, wb_ref, wo_ref, o_ref = refs[n_x + 2 + 2 * N_BRANCH:]
    is_ctx = _is_ctx_tile()
    x = _load_x(x_refs)
    h = _rms(x, g_ref[...]) * (1.0 + mod_ref[1:2, :]) + mod_ref[0:1, :]
    hb = h.astype(BF16)
    merged = None
    for k in range(N_BRANCH):
        cols = slice(k * D_MODEL, (k + 1) * D_MODEL)
        gate = _sigmoid(_dot(hb, wg_ref[:, cols]) + bg_ref[:, cols])
        branch = jnp.where(is_ctx, ctx_refs[k][...], lat_refs[k][...])
        term = gate * _dot(branch, wb_ref[k])
        merged = term if merged is None else merged + term
    o_ref[...] = x + mod_ref[2:3, :] * _dot(merged.astype(BF16), wo_ref[...])


def _merge(x, mod, layer, g, ctx_branches, lat_branches, wg, bg, wb, wo):
    return pl.pallas_call(
        functools.partial(_merge_kernel, n_x=len(_x_args(x))),
        grid=(N_TOK // TOKEN_TILE,),
        in_specs=_x_specs(x) + [_mod_spec(layer), _const_spec((1, D_MODEL))]
                 + [_ctx_tile_spec(BRANCH_DIM)] * N_BRANCH + [_lat_tile_spec(BRANCH_DIM)] * N_BRANCH
                 + [_const_spec((D_MODEL, N_BRANCH * D_MODEL)), _const_spec((1, N_BRANCH * D_MODEL)),
                    _const_spec((N_BRANCH, BRANCH_DIM, D_MODEL)), _const_spec((D_MODEL, D_MODEL))],
        out_specs=_tok_spec(D_MODEL),
        out_shape=jax.ShapeDtypeStruct((N_TOK, D_MODEL), F32),
        compiler_params=_params(("arbitrary",), VMEM_LIMIT),
        name="merge",
    )(*_x_args(x), mod, g, *ctx_branches, *lat_branches, wg, bg, wb, wo)


def _ffn_kernel(x_ref, mod_ref, g_ref, wi_ref, wo_ref, fg_ref, *o_refs, final):
    x = x_ref[...]
    h = _rms(x, g_ref[...]) * (1.0 + mod_ref[4:5, :]) + mod_ref[3:4, :]
    hb = h.astype(BF16)
    a = _dot(hb, wi_ref[:, :D_FF])
    b = _dot(hb, wi_ref[:, D_FF:])
    f = (a * _sigmoid(a)) * b
    y = x + mod_ref[5:6, :] * _dot(f.astype(BF16), wo_ref[...])
    if not final:
        o_refs[0][...] = y
        return
    y = _rms(y, fg_ref[...])
    ctx_ref, lat_ref = o_refs

    @pl.when(_is_ctx_tile())
    def _():
        ctx_ref[...] = y

    @pl.when(jnp.logical_not(_is_ctx_tile()))
    def _():
        lat_ref[...] = y


def _ffn(x, mod, layer, g, wi, wo, fg, final):
    if final:
        out_specs = [_ctx_tile_spec(D_MODEL), _lat_tile_spec(D_MODEL)]
        out_shape = [jax.ShapeDtypeStruct((N_CTX_TOK, D_MODEL), F32),
                     jax.ShapeDtypeStruct((N_LAT_TOK, D_MODEL), F32)]
    else:
        out_specs = _tok_spec(D_MODEL)
        out_shape = jax.ShapeDtypeStruct((N_TOK, D_MODEL), F32)
    return pl.pallas_call(
        functools.partial(_ffn_kernel, final=final),
        grid=(N_TOK // TOKEN_TILE,),
        in_specs=[_tok_spec(D_MODEL), _mod_spec(layer), _const_spec((1, D_MODEL)),
                  _const_spec((D_MODEL, 2 * D_FF)), _const_spec((D_FF, D_MODEL)),
                  _const_spec((1, D_MODEL))],
        out_specs=out_specs,
        out_shape=out_shape,
        compiler_params=_params(("arbitrary",), VMEM_LIMIT),
        name="ffn",
    )(x, mod, g, wi, wo, fg)


@functools.lru_cache(maxsize=None)
def _dft_tables(n, blocks):
    k = np.arange(n, dtype=np.int64)
    ang = 2.0 * np.pi * ((k[:, None] * k[None, :]) % n).astype(np.float64) / n
    out = []
    for m in (np.cos(ang), np.sin(ang)):
        m = np.kron(np.eye(blocks), m / math.sqrt(n)).astype(np.float32)
        hi = m.astype(BF16)
        lo = (m - hi.astype(np.float32)).astype(BF16)
        out += [hi, lo]
    return tuple(out)


def _fourier_kernel(u_ref, ch_ref, cl_ref, sh_ref, sl_ref, cch_ref, ccl_ref, sch_ref, scl_ref, o_ref):
    u_hi, u_lo = _split(u_ref[...])
    a_hi, a_lo = _split(_dot3(ch_ref[...], cl_ref[...], u_hi, u_lo))
    b_hi, b_lo = _split(_dot3(sh_ref[...], sl_ref[...], u_hi, u_lo))
    y = _dot3(a_hi, a_lo, cch_ref[...], ccl_ref[...]) - _dot3(b_hi, b_lo, sch_ref[...], scl_ref[...])
    o_ref[...] = y.astype(o_ref.dtype)


def _fourier(p, seq, n_seq, first_tok):
    tf = 256
    nf = seq // tf
    width = 256
    first_blk = first_tok // seq
    pos = [jnp.asarray(t) for t in _dft_tables(seq, 1)]
    chan = [jnp.asarray(t) for t in _dft_tables(width // 4, 4)]
    pos_spec = pl.BlockSpec((tf, seq), lambda f, b: (f, 0))
    chan_spec = pl.BlockSpec((width, width), lambda f, b: (0, 0))
    return pl.pallas_call(
        _fourier_kernel,
        grid=(nf, n_seq),
        in_specs=[pl.BlockSpec((seq, width), lambda f, b: (first_blk + b, FNET_U // width))]
                 + [pos_spec] * 4 + [chan_spec] * 4,
        out_specs=pl.BlockSpec((tf, width), lambda f, b: (b * nf + f, 0)),
        out_shape=jax.ShapeDtypeStruct((n_seq * seq, width), BF16),
        compiler_params=_params(("arbitrary", "arbitrary"), VMEM_LIMIT),
        name="fourier",
    )(p, *pos, *chan)


FFT_RADIX = 8
FFT_INNER = DEC_SEQ // FFT_RADIX


@functools.lru_cache(maxsize=None)
def _twiddle_tables():
    r = np.arange(FFT_RADIX, dtype=np.float64)[:, None]
    f2 = np.arange(FFT_INNER, dtype=np.float64)[None, :]
    ang = 2.0 * np.pi * r * f2 / DEC_SEQ
    scale = 1.0 / math.sqrt(FFT_RADIX)
    shape = (FFT_RADIX, FFT_INNER, 256)
    return tuple(np.ascontiguousarray(np.broadcast_to((f(ang) * scale)[:, :, None], shape)).astype(np.float32)
                 for f in (np.cos, np.sin))


def _fourier_latent_kernel(u0_ref, u1_ref, ch_ref, cl_ref, sh_ref, sl_ref, twc_ref, tws_ref, cch_ref, ccl_ref,
                           sch_ref, scl_ref, o_ref, tre_ref, tim_ref):
    for r in range(FFT_RADIX):
        rows = pl.ds(r, FFT_INNER, stride=FFT_RADIX)
        x_hi, x_lo = _split(jnp.concatenate([u0_ref[rows, :], u1_ref[rows, :]], axis=1))
        g_re = _dot3(ch_ref[...], cl_ref[...], x_hi, x_lo)
        g_im = -_dot3(sh_ref[...], sl_ref[...], x_hi, x_lo)
        wc, ws = twc_ref[r], tws_ref[r]
        tre_ref[r] = g_re * wc + g_im * ws
        tim_ref[r] = g_im * wc - g_re * ws

    def axpy(acc, coef, x):
        if abs(coef) < 1e-9:
            return acc
        term = x if abs(coef - 1.0) < 1e-9 else (-x if abs(coef + 1.0) < 1e-9 else coef * x)
        return term if acc is None else acc + term

    for f1 in range(FFT_RADIX):
        p_re = p_im = None
        for r in range(FFT_RADIX):
            ang = 2.0 * math.pi * ((f1 * r) % FFT_RADIX) / FFT_RADIX
            a, b = math.cos(ang), math.sin(ang)
            t_re, t_im = tre_ref[r], tim_ref[r]
            p_re = axpy(axpy(p_re, a, t_re), b, t_im)
            p_im = axpy(axpy(p_im, a, t_im), -b, t_re)
        re_hi, re_lo = _split(p_re)
        im_hi, im_lo = _split(p_im)
        y = _dot3(re_hi, re_lo, cch_ref[...], ccl_ref[...]) + _dot3(im_hi, im_lo, sch_ref[...], scl_ref[...])
        o_ref[f1 * FFT_INNER:(f1 + 1) * FFT_INNER, :] = y.astype(o_ref.dtype)


def _fourier_latent(p):
    width = 256
    first_blk = N_CTX_TOK // DEC_SEQ
    inner = [jnp.asarray(t) for t in _dft_tables(FFT_INNER, 1)]
    chan = [jnp.asarray(t) for t in _dft_tables(width // 4, 4)]
    tw = [jnp.asarray(t) for t in _twiddle_tables()]

    def u_spec(half):
        return pl.BlockSpec((DEC_SEQ, LANES), lambda b: (first_blk + b, FNET_U // LANES + half))

    def table(shape):
        nd = len(shape)
        return pl.BlockSpec(shape, lambda b: (0,) * nd, pipeline_mode=pl.Buffered(1))

    return pl.pallas_call(
        _fourier_latent_kernel,
        grid=(DEC_BATCH,),
        in_specs=[u_spec(0), u_spec(1)] + [table((FFT_INNER, FFT_INNER))] * 4
                 + [table((FFT_RADIX, FFT_INNER, width))] * 2 + [table((width, width))] * 4,
        out_specs=pl.BlockSpec((DEC_SEQ, width), lambda b: (b, 0)),
        out_shape=jax.ShapeDtypeStruct((N_LAT_TOK, width), BF16),
        scratch_shapes=[pltpu.VMEM((FFT_RADIX, FFT_INNER, width), F32)] * 2,
        compiler_params=_params(("arbitrary",), VMEM_LIMIT),
        name="fourier_lat",
    )(p, p, *inner, *tw, *chan)


def _ctx_mixer_kernel(p_ref, sink_ref, lp_ref, sg_ref, na_ref, swa_ref, diff_ref, *, lam_init):
    shape = (SEQ, LANES)
    lane = _lane(shape)
    half = lane // HEAD_DIM
    quarter = lane // DIFF_QK_DIM
    scale = HEAD_DIM ** -0.5 * LOG2E

    def pair(col, j):
        return p_ref[:, col + LANES * j: col + LANES * (j + 1)]

    for j in range(2):
        q2 = pair(NA_Q, j) * scale
        kb = pair(NA_K, j).astype(BF16)
        vb = pair(NA_V, j).astype(BF16)
        outs = []
        for g in range(2):
            qm = jnp.where(half == g, q2, 0.0).astype(BF16)
            (e,), l = _exp_parts([_dot_nt(qm, kb)])
            outs.append(_dot(e.astype(BF16), vb) / l)
        na_ref[:, LANES * j: LANES * (j + 1)] = jnp.where(half == 0, outs[0], outs[1]).astype(na_ref.dtype)

    k2 = p_ref[:, SWA_K: SWA_K + LANES]
    v2 = p_ref[:, SWA_V: SWA_V + LANES]
    k2s = pltpu.roll(k2, HEAD_DIM, 1)
    v2s = pltpu.roll(v2, HEAD_DIM, 1)
    for j in range(2):
        kb = jnp.where(half == j, k2, k2s).astype(BF16)
        vb = jnp.where(half == j, v2, v2s).astype(BF16)
        q2 = pair(SWA_Q, j) * scale
        outs = []
        for g in range(2):
            qm = jnp.where(half == g, q2, 0.0).astype(BF16)
            (e,), l = _exp_parts([_dot_nt(qm, kb)], extra=sink_ref[2 * j + g] * LOG2E)
            outs.append(_dot(e.astype(BF16), vb) / l)
        swa_ref[:, LANES * j: LANES * (j + 1)] = jnp.where(half == 0, outs[0], outs[1]).astype(swa_ref.dtype)

    lam = _lambda(lp_ref[...], lam_init)
    for j in range(2):
        q2 = pair(DIFF_Q, j) * (DIFF_QK_DIM ** -0.5 * LOG2E)
        kb = pair(DIFF_K, j).astype(BF16)
        vb = pair(DIFF_V, j).astype(BF16)
        outs = []
        for g in range(2):
            o = []
            for c in range(2):
                qm = jnp.where(quarter == 2 * g + c, q2, 0.0).astype(BF16)
                (e,), l = _exp_parts([_dot_nt(qm, kb)])
                o.append(_dot(e.astype(BF16), vb) / l)
            outs.append(o[0] - lam * o[1])
        o2 = jnp.where(half == 0, outs[0], outs[1])
        diff_ref[:, LANES * j: LANES * (j + 1)] = _subln(o2, sg_ref[...], lam_init).astype(diff_ref.dtype)


def _ctx_mixers(p, sink, lp, sg2, lam_init):
    out = jax.ShapeDtypeStruct((N_CTX_TOK, 256), BF16)
    ospec = pl.BlockSpec((SEQ, 256), lambda b: (b, 0))
    return pl.pallas_call(
        functools.partial(_ctx_mixer_kernel, lam_init=lam_init),
        grid=(BATCH,),
        in_specs=[pl.BlockSpec((SEQ, D_IN), lambda b: (b, 0)),
                  pl.BlockSpec(memory_space=pltpu.SMEM),
                  pl.BlockSpec((4, DIFF_QK_DIM), lambda b: (0, 0)),
                  pl.BlockSpec((1, LANES), lambda b: (0, 0))],
        out_specs=[ospec, ospec, ospec],
        out_shape=[out, out, out],
        compiler_params=_params(("arbitrary",), VMEM_LIMIT),
        name="ctx_mixers",
    )(p, sink, lp, sg2)


N_DR = 2 * NA_WIN_H - 1
N_DC = 2 * NA_WIN_W - 1
N_PAIR_BLOCKS = N_DR + 1


def _na_band_start(t):
    return jnp.clip(NA_TILE_ROWS * t - NA_WIN_H // 2, 0, GRID_H - NA_BAND_ROWS)


def _na_bias_rows(rpb):
    rows = jnp.pad(rpb, ((0, 0), (1, 1), (0, GRID_W - N_DC)))
    return jnp.concatenate([rows[:, :N_PAIR_BLOCKS], rows[:, 1:]], axis=-1)


def _lat_na_kernel(q_ref, k_ref, v_ref, ck_ref, cv_ref, rows_ref, o_ref, pair_ref, keys_ref, vals_ref):
    t = pl.program_id(2)
    blk = (GRID_W, LANES)

    @pl.when(t == 0)
    def _():
        c = lax.broadcasted_iota(jnp.int32, blk, 0)
        kc = _lane(blk) % GRID_W
        cs = jnp.clip(c - NA_WIN_W // 2, 0, GRID_W - NA_WIN_W)
        in_cols = (kc >= cs) & (kc < cs + NA_WIN_W)
        for g in range(2):
            for i in range(N_PAIR_BLOCKS):
                row = jnp.broadcast_to(rows_ref[g, i:i + 1, :], blk) * LOG2E
                toeplitz = pltpu.roll(row, LANES - (NA_WIN_W - 1), 1, stride=1, stride_axis=0)
                pair_ref[g, i] = jnp.where(in_cols, toeplitz, NEG_INF)

        _stage_keys_values(keys_ref, vals_ref, k_ref[...], v_ref[...], ck_ref[...], cv_ref[...])

    band = _na_band_start(t)
    start = pl.multiple_of(band * GRID_W, GRID_W)
    nk = NA_BAND_ROWS * GRID_W
    q2 = q_ref[...] * (HEAD_DIM ** -0.5 * LOG2E)

    def bias(g):
        rows = []
        for ri in range(NA_TILE_ROWS):
            r = NA_TILE_ROWS * t + ri
            first = jnp.clip(r - NA_WIN_H // 2, 0, GRID_H - NA_WIN_H) - band
            d0 = band - r + NA_WIN_H - 1
            blocks = []
            for m in range(NA_BAND_ROWS // 2):
                idx = jnp.clip(d0 + 2 * m + 1, 0, N_PAIR_BLOCKS - 1)
                jrow = 2 * m + _lane(blk) // GRID_W
                in_rows = (jrow >= first) & (jrow < first + NA_WIN_H)
                blocks.append(jnp.where(in_rows, pair_ref[g, idx], NEG_INF))
            rows.append(jnp.concatenate(blocks, axis=1))
        return jnp.concatenate(rows, axis=0)

    o_ref[...] = _attend_pair(q2, keys_ref, vals_ref, start, nk, lambda g, s: s + bias(g)).astype(o_ref.dtype)


def _stage_keys_values(keys_ref, vals_ref, k2, v2, ck2, cv2):
    keys_ref[:DEC_SEQ, :] = k2.astype(BF16)
    keys_ref[DEC_SEQ:, :] = ck2.astype(BF16)
    for g in range(2):
        vals_ref[g, :DEC_SEQ, :] = jnp.where(_lane(v2.shape) // HEAD_DIM == g, v2, 1.0).astype(BF16)
        vals_ref[g, DEC_SEQ:, :] = jnp.where(_lane(cv2.shape) // HEAD_DIM == g, cv2, 1.0).astype(BF16)


def _attend_pair(q2, keys_ref, vals_ref, start, n_loc, fix_local, sink=None):
    half = _lane(q2.shape) // HEAD_DIM
    loc = pl.ds(start, n_loc)
    ctx = slice(DEC_SEQ, DEC_SEQ + PAST_LEN)
    scores = []
    for g in range(2):
        qm = jnp.where(half == g, q2, 0.0).astype(BF16)
        scores.append((fix_local(g, _dot_nt(qm, keys_ref[loc, :])), _dot_nt(qm, keys_ref[ctx, :])))
    outs = []
    for g in range(2):
        s_loc, s_ctx = scores[g]
        m = jnp.maximum(jnp.max(s_loc, axis=-1, keepdims=True), jnp.max(s_ctx, axis=-1, keepdims=True))
        if sink is not None:
            m = jnp.maximum(m, sink[g])
        acc = (_dot(jnp.exp2(s_loc - m).astype(BF16), vals_ref[g, loc, :])
               + _dot(jnp.exp2(s_ctx - m).astype(BF16), vals_ref[g, ctx, :]))
        if sink is not None:
            acc = acc + jnp.where(half == g, 0.0, jnp.exp2(sink[g] - m))
        outs.append(acc / jnp.where(half == g, pltpu.roll(acc, HEAD_DIM, 1), 1.0))
    return jnp.where(half == 0, outs[0], outs[1])


def _staging_scratch():
    n = DEC_SEQ + PAST_LEN
    return [pltpu.VMEM((n, LANES), BF16), pltpu.VMEM((2, n, LANES), BF16)]


def _lat_blocks(col):
    qt = DEC_SEQ // Q_TILE
    first_q = N_CTX_TOK // Q_TILE
    first_k = N_CTX_TOK // DEC_SEQ
    q = pl.BlockSpec((Q_TILE, LANES), lambda b, j, t: (first_q + qt * b + t, col[0] // LANES + j))
    k = pl.BlockSpec((DEC_SEQ, LANES), lambda b, j, t: (first_k + b, col[1] // LANES + j))
    v = pl.BlockSpec((DEC_SEQ, LANES), lambda b, j, t: (first_k + b, col[2] // LANES + j))
    o = pl.BlockSpec((Q_TILE, LANES), lambda b, j, t: (qt * b + t, j))
    return q, k, v, o


def _cache_spec(layer, which, shared_kv):
    return pl.BlockSpec((None, None, None, PAST_LEN, LANES),
                        lambda b, j, t: (b, layer, which, 0, 0 if shared_kv else j))


def _lat_na(p, cache, bias_rows, layer):
    q, k, v, o = _lat_blocks((NA_Q, NA_K, NA_V))
    rows_spec = pl.BlockSpec((2, N_PAIR_BLOCKS, LANES), lambda b, j, t: (j, 0, 0))
    return pl.pallas_call(
        _lat_na_kernel,
        grid=(DEC_BATCH, 2, DEC_SEQ // Q_TILE),
        in_specs=[q, k, v, _cache_spec(layer, 0, False), _cache_spec(layer, 1, False), rows_spec],
        out_specs=o,
        out_shape=jax.ShapeDtypeStruct((N_LAT_TOK, 256), BF16),
        scratch_shapes=[pltpu.VMEM((2, N_PAIR_BLOCKS, GRID_W, LANES), F32)] + _staging_scratch(),
        compiler_params=_params(("arbitrary",) * 3, VMEM_LIMIT),
        name="lat_na",
    )(p, p, p, cache, cache, bias_rows)


@functools.lru_cache(maxsize=None)
def _rope_tables(dim):
    quarter = dim // 4
    pos = np.arange(DEC_SEQ)
    rows, cols = pos // GRID_W, pos % GRID_W
    lane = np.arange(LANES)
    w = lane % dim
    axis_pos = np.where((w // (dim // 2) == 0)[None, :], rows[:, None], cols[:, None]).astype(np.float64)
    u = w % (dim // 2)
    inv = ROPE_BASE ** (-(u % quarter).astype(np.float64) * 2.0 / (dim // 2))
    ang = axis_pos * inv[None, :]
    sign = np.where(u < quarter, -1.0, 1.0)[None, :]
    return np.cos(ang).astype(np.float32), (np.sin(ang) * sign).astype(np.float32)


def _swa_key_start(t):
    return jnp.clip(Q_TILE * t - SWA_WINDOW, 0, DEC_SEQ - SWA_KEYS)


def _lat_swa_kernel(q_ref, k_ref, v_ref, ck_ref, cv_ref, cos_ref, sin_ref, sink_ref, o_ref, keys_ref, vals_ref):
    j = pl.program_id(1)
    t = pl.program_id(2)
    quarter = HEAD_DIM // 4

    @pl.when(t == 0)
    def _():
        def head_j(x):
            return jnp.where(_lane(x.shape) // HEAD_DIM == j, x, pltpu.roll(x, HEAD_DIM, 1))

        _stage_keys_values(keys_ref, vals_ref, head_j(_rope(k_ref[...], cos_ref[...], sin_ref[...], quarter)),
                           head_j(v_ref[...]), head_j(ck_ref[...]), head_j(cv_ref[...]))

    q0 = pl.multiple_of(t * Q_TILE, Q_TILE)
    k0 = pl.multiple_of(_swa_key_start(t), SWA_WINDOW)
    q2 = _rope(q_ref[...], cos_ref[pl.ds(q0, Q_TILE), :], sin_ref[pl.ds(q0, Q_TILE), :], quarter)
    q2 = q2 * (HEAD_DIM ** -0.5 * LOG2E)
    qpos = q0 + lax.broadcasted_iota(jnp.int32, (Q_TILE, SWA_KEYS), 0)
    kpos = k0 + lax.broadcasted_iota(jnp.int32, (Q_TILE, SWA_KEYS), 1)
    valid = jnp.abs(kpos - qpos) <= SWA_WINDOW
    sink = [sink_ref[2 * j + g] * LOG2E for g in range(2)]
    out = _attend_pair(q2, keys_ref, vals_ref, k0, SWA_KEYS, lambda g, s: jnp.where(valid, s, NEG_INF), sink)
    o_ref[...] = out.astype(o_ref.dtype)


def _lat_swa(p, cache, sink, layer):
    q, _, _, o = _lat_blocks((SWA_Q, SWA_K, SWA_V))
    first_k = N_CTX_TOK // DEC_SEQ
    k = pl.BlockSpec((DEC_SEQ, LANES), lambda b, j, t: (first_k + b, SWA_K // LANES))
    v = pl.BlockSpec((DEC_SEQ, LANES), lambda b, j, t: (first_k + b, SWA_V // LANES))
    cos, sin = (jnp.asarray(a) for a in _rope_tables(HEAD_DIM))
    tab = pl.BlockSpec((DEC_SEQ, LANES), lambda b, j, t: (0, 0))
    return pl.pallas_call(
        _lat_swa_kernel,
        grid=(DEC_BATCH, 2, DEC_SEQ // Q_TILE),
        in_specs=[q, k, v, _cache_spec(layer, 0, True), _cache_spec(layer, 1, True), tab, tab,
                  pl.BlockSpec(memory_space=pltpu.SMEM)],
        out_specs=o,
        out_shape=jax.ShapeDtypeStruct((N_LAT_TOK, 256), BF16),
        scratch_shapes=_staging_scratch(),
        compiler_params=_params(("arbitrary",) * 3, VMEM_LIMIT),
        name="lat_swa",
    )(p, p, p, cache, cache, cos, sin, sink)


DIFF_KEYS = DEC_SEQ + PAST_LEN
KEY_CHUNK = 256


def _lat_diff_kernel(q_ref, k_ref, v_ref, ck_ref, cv_ref, cos_ref, sin_ref, lp_ref, sg_ref, o_ref,
                     keys_ref, vals_ref, s_ref, *, lam_init):
    t = pl.program_id(2)
    eighth = DIFF_QK_DIM // 4

    @pl.when(t == 0)
    def _():
        _stage_keys_values(keys_ref, vals_ref, _rope(k_ref[...], cos_ref[...], sin_ref[...], eighth), v_ref[...],
                           ck_ref[...], cv_ref[...])

    q0 = pl.multiple_of(t * Q_TILE, Q_TILE)
    q2 = _rope(q_ref[...], cos_ref[pl.ds(q0, Q_TILE), :], sin_ref[pl.ds(q0, Q_TILE), :], eighth)
    q2 = q2 * (DIFF_QK_DIM ** -0.5 * LOG2E)
    lam = _lambda(lp_ref[...], lam_init)
    lane = _lane(q2.shape)
    quarter = lane // DIFF_QK_DIM
    half = lane // HEAD_DIM
    n_chunks = DIFF_KEYS // KEY_CHUNK

    def scores(i):
        qm = jnp.where(quarter == i, q2, 0.0).astype(BF16)
        m_run = None
        for n in range(n_chunks):
            cols = slice(n * KEY_CHUNK, (n + 1) * KEY_CHUNK)
            s = _dot_nt(qm, keys_ref[cols, :])
            s_ref[i, :, cols] = s
            m_blk = jnp.maximum(s[:, :LANES], s[:, LANES:])
            m_run = m_blk if m_run is None else jnp.maximum(m_run, m_blk)
        return jnp.max(m_run, axis=-1, keepdims=True)

    def values(i, m):
        g = i // 2
        acc = None
        for n in range(n_chunks):
            cols = slice(n * KEY_CHUNK, (n + 1) * KEY_CHUNK)
            part = _dot(jnp.exp2(s_ref[i, :, cols] - m).astype(BF16), vals_ref[g, cols, :])
            acc = part if acc is None else acc + part
        return acc / jnp.where(half == g, pltpu.roll(acc, HEAD_DIM, 1), 1.0)

    o = []
    m_next = scores(0)
    for i in range(4):
        m = m_next
        if i + 1 < 4:
            m_next = scores(i + 1)
        o.append(values(i, m))
    outs = [o[0] - lam * o[1], o[2] - lam * o[3]]
    o2 = jnp.where(half == 0, outs[0], outs[1])
    o_ref[...] = _subln(o2, sg_ref[...], lam_init).astype(o_ref.dtype)


def _lat_diff(p, cache, lp, sg2, layer, lam_init):
    q, k, v, o = _lat_blocks((DIFF_Q, DIFF_K, DIFF_V))
    cos, sin = (jnp.asarray(a) for a in _rope_tables(DIFF_QK_DIM))
    tab = pl.BlockSpec((DEC_SEQ, LANES), lambda b, j, t: (0, 0))
    return pl.pallas_call(
        functools.partial(_lat_diff_kernel, lam_init=lam_init),
        grid=(DEC_BATCH, 2, DEC_SEQ // Q_TILE),
        in_specs=[q, k, v, _cache_spec(layer, 0, False), _cache_spec(layer, 1, False), tab, tab,
                  pl.BlockSpec((4, DIFF_QK_DIM), lambda b, j, t: (0, 0)),
                  pl.BlockSpec((1, LANES), lambda b, j, t: (0, 0))],
        out_specs=o,
        out_shape=jax.ShapeDtypeStruct((N_LAT_TOK, 256), BF16),
        scratch_shapes=_staging_scratch() + [pltpu.VMEM((4, Q_TILE, DIFF_KEYS), F32)],
        compiler_params=_params(("arbitrary",) * 3, VMEM_LIMIT),
        name="lat_diff",
    )(p, p, p, cache, cache, cos, sin, lp, sg2)


def kernel(x_prompt, x_sample, cache_na_kv, cache_swa_kv, cache_diff_kv, c, c_ctx, norm1_g, norm2_g, ada_w,
           ada_b, w_in, na_rpb, swa_sink, diff_lambda, diff_subln_g, w_branch, w_gate, b_gate, w_o, w_ffn_in,
           w_ffn_out, final_norm_g):
    x = (x_prompt.reshape(N_CTX_TOK, D_MODEL), x_sample.reshape(N_LAT_TOK, D_MODEL))
    cond = jnp.zeros((COND_ROWS, D_MODEL), F32).at[0].set(c_ctx).at[1:1 + DEC_BATCH].set(c)
    mod = _adaln(cond, ada_w, ada_b).reshape(DEPTH, COND_ROWS, 6, D_MODEL)

    cache_na = cache_na_kv.reshape(DEC_BATCH, DEPTH, 2, PAST_LEN, 256)
    cache_swa = cache_swa_kv.reshape(DEC_BATCH, DEPTH, 2, PAST_LEN, LANES)
    cache_diff = cache_diff_kv.reshape(DEC_BATCH, DEPTH, 2, PAST_LEN, 256)
    fg = final_norm_g.reshape(1, D_MODEL)

    states = None
    for l in range(DEPTH):
        lam_init = 0.8 - 0.6 * math.exp(-0.3 * l)
        g1 = norm1_g[l].reshape(1, D_MODEL)
        sg2 = jnp.tile(diff_subln_g[l], 2).reshape(1, LANES)
        p, *states = _proj(x, mod, l, g1, w_in[l].astype(BF16), states)

        c_na, c_swa, c_diff = _ctx_mixers(p, swa_sink[l], diff_lambda[l], sg2, lam_init)
        c_f = _fourier(p, SEQ, BATCH, 0)
        l_na = _lat_na(p, cache_na, _na_bias_rows(na_rpb[l]), l)
        l_swa = _lat_swa(p, cache_swa, swa_sink[l], l)
        l_f = _fourier_latent(p)
        l_diff = _lat_diff(p, cache_diff, diff_lambda[l], sg2, l, lam_init)

        x = _merge(x, mod, l, g1, (c_na, c_swa, c_f, c_diff), (l_na, l_swa, l_f, l_diff),
                   w_gate[l].astype(BF16), b_gate[l].reshape(1, -1), w_branch[l].astype(BF16),
                   w_o[l].astype(BF16))
        x = _ffn(x, mod, l, norm2_g[l].reshape(1, D_MODEL), w_ffn_in[l].astype(BF16),
                 w_ffn_out[l].astype(BF16), fg, final=(l == DEPTH - 1))

    y_ctx, y_lat = x
    na_kv, swa_kv, diff_kv = states
    return (y_ctx.reshape(BATCH, SEQ, D_MODEL), y_lat.reshape(DEC_BATCH, DEC_SEQ, D_MODEL),
            na_kv.reshape(BATCH, DEPTH, 2, SEQ, 4, HEAD_DIM), swa_kv.reshape(BATCH, DEPTH, 2, SEQ, 2, HEAD_DIM),
            diff_kv.reshape(BATCH, DEPTH, 2, SEQ, 4, HEAD_DIM))
```

```python
import functools
import math

import numpy as np
import jax
import jax.numpy as jnp
from jax import lax
from jax.experimental import pallas as pl
from jax.experimental.pallas import tpu as pltpu

D_MODEL = 1024
BATCH = 16
SEQ = 256
DEPTH = 2
DEC_BATCH = 4
DEC_SEQ = 2048
PAST_LEN = 512
GRID_W = 64
GRID_H = DEC_SEQ // GRID_W
HEAD_DIM = 64
NA_WIN_H = 8
NA_WIN_W = 16
SWA_WINDOW = 128
DIFF_QK_DIM = 32
D_FF = 2816
D_IN = 2304
ROPE_BASE = 10000.0
NORM_EPS = 1e-6
NEG_INF = -1e30
LOG2E = math.log2(math.e)

NA_Q, NA_K, NA_V = 0, 256, 512
SWA_Q, SWA_K, SWA_V = 768, 1024, 1152
FNET_U = 1280
DIFF_Q, DIFF_K, DIFF_V = 1536, 1792, 2048

LANES = 128
N_CTX_TOK = BATCH * SEQ
N_LAT_TOK = DEC_BATCH * DEC_SEQ
N_TOK = N_CTX_TOK + N_LAT_TOK
COND_ROWS = 8
TOKEN_TILE = 512
Q_TILE = 256
NA_TILE_ROWS = Q_TILE // GRID_W
NA_BAND_ROWS = NA_WIN_H + NA_TILE_ROWS
SWA_KEYS = 2 * Q_TILE
VMEM_LIMIT = 56 * 1024 * 1024

F32 = jnp.float32
BF16 = jnp.bfloat16


def _params(semantics, vmem=None):
    return pltpu.CompilerParams(dimension_semantics=semantics, vmem_limit_bytes=vmem)


def _dot(a, b):
    return jnp.dot(a, b, preferred_element_type=F32)


def _dot_nt(a, b):
    return lax.dot_general(a, b, (((1,), (1,)), ((), ())), preferred_element_type=F32)


def _split(x):
    hi = x.astype(BF16)
    lo = (x - hi.astype(F32)).astype(BF16)
    return hi, lo


def _dot3(a_hi, a_lo, b_hi, b_lo):
    return _dot(a_hi, b_hi) + _dot(a_lo, b_hi) + _dot(a_hi, b_lo)


def _sigmoid(x):
    return 1.0 / (1.0 + jnp.exp(-x))


def _rms(x, g):
    return x * lax.rsqrt(jnp.mean(x * x, axis=-1, keepdims=True) + NORM_EPS) * g


def _exp_parts(blocks, extra=None):
    m = None
    for s in blocks:
        mi = jnp.max(s, axis=-1, keepdims=True)
        m = mi if m is None else jnp.maximum(m, mi)
    if extra is not None:
        m = jnp.maximum(m, extra)
    es = [jnp.exp2(s - m) for s in blocks]
    l = None
    for e in es:
        li = jnp.sum(e, axis=-1, keepdims=True)
        l = li if l is None else l + li
    if extra is not None:
        l = l + jnp.exp2(extra - m)
    return es, l


def _lane(shape):
    return lax.broadcasted_iota(jnp.int32, shape, 1)


def _rope(x, cos, sin_signed, half):
    lane = _lane(x.shape)
    partner = jnp.where((lane % (2 * half)) < half,
                        pltpu.roll(x, LANES - half, 1), pltpu.roll(x, half, 1))
    return x * cos + partner * sin_signed


def _lambda(lp, lam_init):
    s1 = jnp.sum(lp[0:1, :] * lp[1:2, :], axis=-1, keepdims=True)
    s2 = jnp.sum(lp[2:3, :] * lp[3:4, :], axis=-1, keepdims=True)
    return jnp.exp(s1) - jnp.exp(s2) + lam_init


def _subln(o, g2, lam_init):
    lane = _lane(o.shape)
    sq = o * o
    ms0 = jnp.sum(jnp.where(lane < HEAD_DIM, sq, 0.0), axis=-1, keepdims=True)
    ms1 = jnp.sum(jnp.where(lane >= HEAD_DIM, sq, 0.0), axis=-1, keepdims=True)
    ms = jnp.where(lane < HEAD_DIM, ms0, ms1) * (1.0 / HEAD_DIM)
    return (o * lax.rsqrt(ms + NORM_EPS) * g2) * (1.0 - lam_init)


def _adaln_kernel(cond_ref, w_ref, b_ref, o_ref):
    c = cond_ref[...]
    s = c * _sigmoid(c)
    s_hi, s_lo = _split(s)
    w_hi, w_lo = _split(w_ref[...])
    o_ref[...] = _dot3(s_hi, s_lo, w_hi, w_lo) + b_ref[...]


def _adaln(cond, ada_w, ada_b):
    tn = 1536
    n = 6 * D_MODEL
    return pl.pallas_call(
        _adaln_kernel,
        grid=(DEPTH, n // tn),
        in_specs=[
            pl.BlockSpec((COND_ROWS, D_MODEL), lambda l, j: (0, 0)),
            pl.BlockSpec((None, D_MODEL, tn), lambda l, j: (l, 0, j)),
            pl.BlockSpec((None, 1, tn), lambda l, j: (l, 0, j)),
        ],
        out_specs=pl.BlockSpec((None, COND_ROWS, tn), lambda l, j: (l, 0, j)),
        out_shape=jax.ShapeDtypeStruct((DEPTH, COND_ROWS, n), F32),
        compiler_params=_params(("arbitrary", "arbitrary")),
        name="adaln",
    )(cond, ada_w, ada_b.reshape(DEPTH, 1, n))


def _cond_row(i):
    n_ctx = N_CTX_TOK // TOKEN_TILE
    per_seq = DEC_SEQ // TOKEN_TILE
    return jnp.where(i < n_ctx, 0, 1 + (i - n_ctx) // per_seq)


def _mod_spec(layer):
    return pl.BlockSpec((None, None, 6, D_MODEL), lambda i: (layer, _cond_row(i), 0, 0))


ROW_SPLIT = 2
N_CTX_TILES = N_CTX_TOK // TOKEN_TILE
N_LAT_TILES = N_LAT_TOK // TOKEN_TILE


def _tok_spec(width):
    return pl.BlockSpec((TOKEN_TILE, width), lambda i: (i, 0))


def _ctx_tile_spec(width):
    return pl.BlockSpec((TOKEN_TILE, width), lambda i: (jnp.minimum(i, N_CTX_TILES - 1), 0))


def _lat_tile_spec(width):
    return pl.BlockSpec((TOKEN_TILE, width), lambda i: (jnp.clip(i - N_CTX_TILES, 0, N_LAT_TILES - 1), 0))


def _const_spec(shape):
    nd = len(shape)
    return pl.BlockSpec(shape, lambda i: (0,) * nd, pipeline_mode=pl.Buffered(1))


def _is_ctx_tile():
    return pl.program_id(0) < N_CTX_TILES


def _x_specs(x):
    return [_ctx_tile_spec(D_MODEL), _lat_tile_spec(D_MODEL)] if isinstance(x, tuple) else [_tok_spec(D_MODEL)]


def _x_args(x):
    return list(x) if isinstance(x, tuple) else [x]


def _load_x(x_refs, rows=slice(None)):
    if len(x_refs) == 1:
        return x_refs[0][rows, :]
    return jnp.where(_is_ctx_tile(), x_refs[0][rows, :], x_refs[1][rows, :])


def _proj_kernel(*refs, n_x):
    x_refs = refs[:n_x]
    mod_ref, g_ref, w_ref = refs[n_x:n_x + 3]
    p_ref, h_ref, na_ref, swa_ref, diff_ref = refs[-5:]
    for r in range(ROW_SPLIT):
        rows = slice(r * TOKEN_TILE // ROW_SPLIT, (r + 1) * TOKEN_TILE // ROW_SPLIT)
        h = _rms(_load_x(x_refs, rows), g_ref[...]) * (1.0 + mod_ref[1:2, :]) + mod_ref[0:1, :]
        hb = h.astype(BF16)
        h_ref[rows, :] = hb
        p_ref[rows, :] = _dot(hb, w_ref[...])

    @pl.when(_is_ctx_tile())
    def _():
        for s in range(TOKEN_TILE // SEQ):
            rows = slice(s * SEQ, (s + 1) * SEQ)
            for which in range(2):
                na_ref[s, which] = p_ref[rows, NA_K + 256 * which: NA_K + 256 * (which + 1)]
                swa_ref[s, which] = p_ref[rows, SWA_K + LANES * which: SWA_K + LANES * (which + 1)]
                diff_ref[s, which] = p_ref[rows, DIFF_K + 256 * which: DIFF_K + 256 * (which + 1)]


STATE_WIDTHS = (256, LANES, 256)


def _proj(x, mod, layer, g, w, states):
    seqs = TOKEN_TILE // SEQ

    def state_spec(width):
        return pl.BlockSpec((seqs, None, 2, SEQ, width),
                            lambda i: (jnp.minimum(i, N_CTX_TILES - 1), layer, 0, 0, 0))

    n_x = len(_x_args(x))
    n_in = n_x + 3
    prior = [] if states is None else list(states)
    return pl.pallas_call(
        functools.partial(_proj_kernel, n_x=n_x),
        grid=(N_TOK // TOKEN_TILE,),
        in_specs=_x_specs(x) + [_mod_spec(layer), _const_spec((1, D_MODEL)), _const_spec((D_MODEL, D_IN))]
                 + [pl.BlockSpec(memory_space=pl.ANY)] * len(prior),
        out_specs=[_tok_spec(D_IN), _tok_spec(D_MODEL)] + [state_spec(w_) for w_ in STATE_WIDTHS],
        out_shape=[jax.ShapeDtypeStruct((N_TOK, D_IN), F32), jax.ShapeDtypeStruct((N_TOK, D_MODEL), BF16)]
                  + [jax.ShapeDtypeStruct((BATCH, DEPTH, 2, SEQ, w_), F32) for w_ in STATE_WIDTHS],
        input_output_aliases={n_in + k: 2 + k for k in range(len(prior))},
        compiler_params=_params(("arbitrary",), VMEM_LIMIT),
        name="proj",
    )(*_x_args(x), mod, g, w, *prior)


N_BRANCH = 4
BRANCH_DIM = D_MODEL // N_BRANCH


def _merge_kernel(*refs, n_x):
    x_refs = refs[:n_x]
    mod_ref, h_ref = refs[n_x:n_x + 2]
    ctx_refs = refs[n_x + 2:n_x + 2 + N_BRANCH]
    lat_refs = refs[n_x + 2 + N_BRANCH:n_x + 2 + 2 * N_BRANCH]
    wg_ref, bg_ref, wb_ref, wo_ref, o_ref = refs[n_x + 2 + 2 * N_BRANCH:]
    is_ctx = _is_ctx_tile()
    for r in range(ROW_SPLIT):
        rows = slice(r * TOKEN_TILE // ROW_SPLIT, (r + 1) * TOKEN_TILE // ROW_SPLIT)
        hb = h_ref[rows, :]
        merged = None
        for k in range(N_BRANCH):
            cols = slice(k * D_MODEL, (k + 1) * D_MODEL)
            gate = _sigmoid(_dot(hb, wg_ref[:, cols]) + bg_ref[:, cols])
            branch = jnp.where(is_ctx, ctx_refs[k][rows, :], lat_refs[k][rows, :])
            term = gate * _dot(branch, wb_ref[k])
            merged = term if merged is None else merged + term
        o_ref[rows, :] = _load_x(x_refs, rows) + mod_ref[2:3, :] * _dot(merged.astype(BF16), wo_ref[...])


def _merge(x, mod, layer, h, ctx_branches, lat_branches, wg, bg, wb, wo):
    return pl.pallas_call(
        functools.partial(_merge_kernel, n_x=len(_x_args(x))),
        grid=(N_TOK // TOKEN_TILE,),
        in_specs=_x_specs(x) + [_mod_spec(layer), _tok_spec(D_MODEL)]
                 + [_ctx_tile_spec(BRANCH_DIM)] * N_BRANCH + [_lat_tile_spec(BRANCH_DIM)] * N_BRANCH
                 + [_const_spec((D_MODEL, N_BRANCH * D_MODEL)), _const_spec((1, N_BRANCH * D_MODEL)),
                    _const_spec((N_BRANCH, BRANCH_DIM, D_MODEL)), _const_spec((D_MODEL, D_MODEL))],
        out_specs=_tok_spec(D_MODEL),
        out_shape=jax.ShapeDtypeStruct((N_TOK, D_MODEL), F32),
        compiler_params=_params(("arbitrary",), VMEM_LIMIT),
        name="merge",
    )(*_x_args(x), mod, h, *ctx_branches, *lat_branches, wg, bg, wb, wo)


def _ffn_kernel(x_ref, mod_ref, g_ref, wi_ref, wo_ref, fg_ref, *o_refs, final):
    ys = []
    for r in range(ROW_SPLIT):
        rows = slice(r * TOKEN_TILE // ROW_SPLIT, (r + 1) * TOKEN_TILE // ROW_SPLIT)
        x = x_ref[rows, :]
        h = _rms(x, g_ref[...]) * (1.0 + mod_ref[4:5, :]) + mod_ref[3:4, :]
        hb = h.astype(BF16)
        a = _dot(hb, wi_ref[:, :D_FF])
        b = _dot(hb, wi_ref[:, D_FF:])
        f = (a * _sigmoid(a)) * b
        y = x + mod_ref[5:6, :] * _dot(f.astype(BF16), wo_ref[...])
        if not final:
            o_refs[0][rows, :] = y
        else:
            ys.append(_rms(y, fg_ref[...]))
    if not final:
        return
    y = jnp.concatenate(ys, axis=0)
    ctx_ref, lat_ref = o_refs

    @pl.when(_is_ctx_tile())
    def _():
        ctx_ref[...] = y

    @pl.when(jnp.logical_not(_is_ctx_tile()))
    def _():
        lat_ref[...] = y


def _ffn(x, mod, layer, g, wi, wo, fg, final):
    if final:
        out_specs = [_ctx_tile_spec(D_MODEL), _lat_tile_spec(D_MODEL)]
        out_shape = [jax.ShapeDtypeStruct((N_CTX_TOK, D_MODEL), F32),
                     jax.ShapeDtypeStruct((N_LAT_TOK, D_MODEL), F32)]
    else:
        out_specs = _tok_spec(D_MODEL)
        out_shape = jax.ShapeDtypeStruct((N_TOK, D_MODEL), F32)
    return pl.pallas_call(
        functools.partial(_ffn_kernel, final=final),
        grid=(N_TOK // TOKEN_TILE,),
        in_specs=[_tok_spec(D_MODEL), _mod_spec(layer), _const_spec((1, D_MODEL)),
                  _const_spec((D_MODEL, 2 * D_FF)), _const_spec((D_FF, D_MODEL)),
                  _const_spec((1, D_MODEL))],
        out_specs=out_specs,
        out_shape=out_shape,
        compiler_params=_params(("arbitrary",), VMEM_LIMIT),
        name="ffn",
    )(x, mod, g, wi, wo, fg)


@functools.lru_cache(maxsize=None)
def _dft_tables(n, blocks):
    k = np.arange(n, dtype=np.int64)
    ang = 2.0 * np.pi * ((k[:, None] * k[None, :]) % n).astype(np.float64) / n
    out = []
    for m in (np.cos(ang), np.sin(ang)):
        m = np.kron(np.eye(blocks), m / math.sqrt(n)).astype(np.float32)
        hi = m.astype(BF16)
        lo = (m - hi.astype(np.float32)).astype(BF16)
        out += [hi, lo]
    return tuple(out)


def _fourier_direct(u, pos_refs, chan_refs):
    ch_ref, cl_ref, sh_ref, sl_ref = pos_refs
    cch_ref, ccl_ref, sch_ref, scl_ref = chan_refs
    u_hi, u_lo = _split(u)
    a_hi, a_lo = _split(_dot3(ch_ref[...], cl_ref[...], u_hi, u_lo))
    b_hi, b_lo = _split(_dot3(sh_ref[...], sl_ref[...], u_hi, u_lo))
    return _dot3(a_hi, a_lo, cch_ref[...], ccl_ref[...]) - _dot3(b_hi, b_lo, sch_ref[...], scl_ref[...])


FFT_RADIX = 8
FFT_INNER = DEC_SEQ // FFT_RADIX


@functools.lru_cache(maxsize=None)
def _twiddle_tables():
    r = np.arange(FFT_RADIX, dtype=np.float64)[:, None]
    f2 = np.arange(FFT_INNER, dtype=np.float64)[None, :]
    ang = 2.0 * np.pi * r * f2 / DEC_SEQ
    scale = 1.0 / math.sqrt(FFT_RADIX)
    shape = (FFT_RADIX, FFT_INNER, 256)
    return tuple(np.ascontiguousarray(np.broadcast_to((f(ang) * scale)[:, :, None], shape)).astype(np.float32)
                 for f in (np.cos, np.sin))


def _fourier_latent_kernel(u0_ref, u1_ref, ch_ref, cl_ref, sh_ref, sl_ref, twc_ref, tws_ref, cch_ref, ccl_ref,
                           sch_ref, scl_ref, o_ref, tre_ref, tim_ref):
    for r in range(FFT_RADIX):
        rows = pl.ds(r, FFT_INNER, stride=FFT_RADIX)
        x_hi, x_lo = _split(jnp.concatenate([u0_ref[rows, :], u1_ref[rows, :]], axis=1))
        g_re = _dot3(ch_ref[...], cl_ref[...], x_hi, x_lo)
        g_im = -_dot3(sh_ref[...], sl_ref[...], x_hi, x_lo)
        wc, ws = twc_ref[r], tws_ref[r]
        tre_ref[r] = g_re * wc + g_im * ws
        tim_ref[r] = g_im * wc - g_re * ws

    def axpy(acc, coef, x):
        if abs(coef) < 1e-9:
            return acc
        term = x if abs(coef - 1.0) < 1e-9 else (-x if abs(coef + 1.0) < 1e-9 else coef * x)
        return term if acc is None else acc + term

    for f1 in range(FFT_RADIX):
        p_re = p_im = None
        for r in range(FFT_RADIX):
            ang = 2.0 * math.pi * ((f1 * r) % FFT_RADIX) / FFT_RADIX
            a, b = math.cos(ang), math.sin(ang)
            t_re, t_im = tre_ref[r], tim_ref[r]
            p_re = axpy(axpy(p_re, a, t_re), b, t_im)
            p_im = axpy(axpy(p_im, a, t_im), -b, t_re)
        re_hi, re_lo = _split(p_re)
        im_hi, im_lo = _split(p_im)
        y = _dot3(re_hi, re_lo, cch_ref[...], ccl_ref[...]) + _dot3(im_hi, im_lo, sch_ref[...], scl_ref[...])
        o_ref[f1 * FFT_INNER:(f1 + 1) * FFT_INNER, :] = y.astype(o_ref.dtype)


def _fourier_latent(p):
    width = 256
    first_blk = N_CTX_TOK // DEC_SEQ
    inner = [jnp.asarray(t) for t in _dft_tables(FFT_INNER, 1)]
    chan = [jnp.asarray(t) for t in _dft_tables(width // 4, 4)]
    tw = [jnp.asarray(t) for t in _twiddle_tables()]

    def u_spec(half):
        return pl.BlockSpec((DEC_SEQ, LANES), lambda b: (first_blk + b, FNET_U // LANES + half))

    def table(shape):
        nd = len(shape)
        return pl.BlockSpec(shape, lambda b: (0,) * nd, pipeline_mode=pl.Buffered(1))

    return pl.pallas_call(
        _fourier_latent_kernel,
        grid=(DEC_BATCH,),
        in_specs=[u_spec(0), u_spec(1)] + [table((FFT_INNER, FFT_INNER))] * 4
                 + [table((FFT_RADIX, FFT_INNER, width))] * 2 + [table((width, width))] * 4,
        out_specs=pl.BlockSpec((DEC_SEQ, width), lambda b: (b, 0)),
        out_shape=jax.ShapeDtypeStruct((N_LAT_TOK, width), BF16),
        scratch_shapes=[pltpu.VMEM((FFT_RADIX, FFT_INNER, width), F32)] * 2,
        compiler_params=_params(("arbitrary",), VMEM_LIMIT),
        name="fourier_lat",
    )(p, p, *inner, *tw, *chan)


def _ctx_mixer_kernel(p_ref, sink_ref, lp_ref, sg_ref, *refs, lam_init):
    pos_refs, chan_refs = refs[:4], refs[4:8]
    na_ref, swa_ref, fnet_ref, diff_ref = refs[8:]
    fnet_ref[...] = _fourier_direct(p_ref[:, FNET_U:FNET_U + 256], pos_refs, chan_refs).astype(fnet_ref.dtype)

    shape = (SEQ, LANES)
    lane = _lane(shape)
    half = lane // HEAD_DIM
    quarter = lane // DIFF_QK_DIM
    scale = HEAD_DIM ** -0.5 * LOG2E

    def pair(col, j):
        return p_ref[:, col + LANES * j: col + LANES * (j + 1)]

    for j in range(2):
        q2 = pair(NA_Q, j) * scale
        kb = pair(NA_K, j).astype(BF16)
        vb = pair(NA_V, j).astype(BF16)
        outs = []
        for g in range(2):
            qm = jnp.where(half == g, q2, 0.0).astype(BF16)
            (e,), l = _exp_parts([_dot_nt(qm, kb)])
            outs.append(_dot(e.astype(BF16), vb) / l)
        na_ref[:, LANES * j: LANES * (j + 1)] = jnp.where(half == 0, outs[0], outs[1]).astype(na_ref.dtype)

    k2 = p_ref[:, SWA_K: SWA_K + LANES]
    v2 = p_ref[:, SWA_V: SWA_V + LANES]
    k2s = pltpu.roll(k2, HEAD_DIM, 1)
    v2s = pltpu.roll(v2, HEAD_DIM, 1)
    for j in range(2):
        kb = jnp.where(half == j, k2, k2s).astype(BF16)
        vb = jnp.where(half == j, v2, v2s).astype(BF16)
        q2 = pair(SWA_Q, j) * scale
        outs = []
        for g in range(2):
            qm = jnp.where(half == g, q2, 0.0).astype(BF16)
            (e,), l = _exp_parts([_dot_nt(qm, kb)], extra=sink_ref[2 * j + g] * LOG2E)
            outs.append(_dot(e.astype(BF16), vb) / l)
        swa_ref[:, LANES * j: LANES * (j + 1)] = jnp.where(half == 0, outs[0], outs[1]).astype(swa_ref.dtype)

    lam = _lambda(lp_ref[...], lam_init)
    for j in range(2):
        q2 = pair(DIFF_Q, j) * (DIFF_QK_DIM ** -0.5 * LOG2E)
        kb = pair(DIFF_K, j).astype(BF16)
        vb = pair(DIFF_V, j).astype(BF16)
        outs = []
        for g in range(2):
            o = []
            for c in range(2):
                qm = jnp.where(quarter == 2 * g + c, q2, 0.0).astype(BF16)
                (e,), l = _exp_parts([_dot_nt(qm, kb)])
                o.append(_dot(e.astype(BF16), vb) / l)
            outs.append(o[0] - lam * o[1])
        o2 = jnp.where(half == 0, outs[0], outs[1])
        diff_ref[:, LANES * j: LANES * (j + 1)] = _subln(o2, sg_ref[...], lam_init).astype(diff_ref.dtype)


def _ctx_mixers(p, sink, lp, sg2, lam_init):
    out = jax.ShapeDtypeStruct((N_CTX_TOK, 256), BF16)
    ospec = pl.BlockSpec((SEQ, 256), lambda b: (b, 0))
    tables = [jnp.asarray(t) for t in _dft_tables(SEQ, 1) + _dft_tables(256 // 4, 4)]
    table_spec = pl.BlockSpec((256, 256), lambda b: (0, 0), pipeline_mode=pl.Buffered(1))
    return pl.pallas_call(
        functools.partial(_ctx_mixer_kernel, lam_init=lam_init),
        grid=(BATCH,),
        in_specs=[pl.BlockSpec((SEQ, D_IN), lambda b: (b, 0)),
                  pl.BlockSpec(memory_space=pltpu.SMEM),
                  pl.BlockSpec((4, DIFF_QK_DIM), lambda b: (0, 0)),
                  pl.BlockSpec((1, LANES), lambda b: (0, 0))] + [table_spec] * 8,
        out_specs=[ospec] * 4,
        out_shape=[out] * 4,
        compiler_params=_params(("arbitrary",), VMEM_LIMIT),
        name="ctx_mixers",
    )(p, sink, lp, sg2, *tables)


N_DR = 2 * NA_WIN_H - 1
N_DC = 2 * NA_WIN_W - 1
N_PAIR_BLOCKS = N_DR + 1


def _na_band_start(t):
    return jnp.clip(NA_TILE_ROWS * t - NA_WIN_H // 2, 0, GRID_H - NA_BAND_ROWS)


def _na_bias_rows(rpb):
    rows = jnp.pad(rpb, ((0, 0), (1, 1), (0, GRID_W - N_DC)))
    return jnp.concatenate([rows[:, :N_PAIR_BLOCKS], rows[:, 1:]], axis=-1)


def _lat_na_kernel(q_ref, k_ref, v_ref, ck_ref, cv_ref, rows_ref, o_ref, pair_ref, keys_ref, vals_ref):
    t = pl.program_id(2)
    blk = (GRID_W, LANES)

    @pl.when(t == 0)
    def _():
        c = lax.broadcasted_iota(jnp.int32, blk, 0)
        kc = _lane(blk) % GRID_W
        cs = jnp.clip(c - NA_WIN_W // 2, 0, GRID_W - NA_WIN_W)
        in_cols = (kc >= cs) & (kc < cs + NA_WIN_W)
        for g in range(2):
            for i in range(N_PAIR_BLOCKS):
                row = jnp.broadcast_to(rows_ref[g, i:i + 1, :], blk) * LOG2E
                toeplitz = pltpu.roll(row, LANES - (NA_WIN_W - 1), 1, stride=1, stride_axis=0)
                pair_ref[g, i] = jnp.where(in_cols, toeplitz, NEG_INF)

        _stage_keys_values(keys_ref, vals_ref, k_ref[...], v_ref[...], ck_ref[...], cv_ref[...])

    band = _na_band_start(t)
    start = pl.multiple_of(band * GRID_W, GRID_W)
    nk = NA_BAND_ROWS * GRID_W
    q2 = q_ref[...] * (HEAD_DIM ** -0.5 * LOG2E)

    def bias(g):
        rows = []
        for ri in range(NA_TILE_ROWS):
            r = NA_TILE_ROWS * t + ri
            first = jnp.clip(r - NA_WIN_H // 2, 0, GRID_H - NA_WIN_H) - band
            d0 = band - r + NA_WIN_H - 1
            blocks = []
            for m in range(NA_BAND_ROWS // 2):
                idx = jnp.clip(d0 + 2 * m + 1, 0, N_PAIR_BLOCKS - 1)
                jrow = 2 * m + _lane(blk) // GRID_W
                in_rows = (jrow >= first) & (jrow < first + NA_WIN_H)
                blocks.append(jnp.where(in_rows, pair_ref[g, idx], NEG_INF))
            rows.append(jnp.concatenate(blocks, axis=1))
        return jnp.concatenate(rows, axis=0)

    o_ref[...] = _attend_pair(q2, keys_ref, vals_ref, start, nk, lambda g, s: s + bias(g)).astype(o_ref.dtype)


def _stage_keys_values(keys_ref, vals_ref, k2, v2, ck2, cv2):
    keys_ref[:DEC_SEQ, :] = k2.astype(BF16)
    keys_ref[DEC_SEQ:, :] = ck2.astype(BF16)
    for g in range(2):
        vals_ref[g, :DEC_SEQ, :] = jnp.where(_lane(v2.shape) // HEAD_DIM == g, v2, 1.0).astype(BF16)
        vals_ref[g, DEC_SEQ:, :] = jnp.where(_lane(cv2.shape) // HEAD_DIM == g, cv2, 1.0).astype(BF16)


def _attend_pair(q2, keys_ref, vals_ref, start, n_loc, fix_local, sink=None):
    half = _lane(q2.shape) // HEAD_DIM
    loc = pl.ds(start, n_loc)
    ctx = slice(DEC_SEQ, DEC_SEQ + PAST_LEN)
    scores = []
    for g in range(2):
        qm = jnp.where(half == g, q2, 0.0).astype(BF16)
        scores.append((fix_local(g, _dot_nt(qm, keys_ref[loc, :])), _dot_nt(qm, keys_ref[ctx, :])))
    outs = []
    for g in range(2):
        s_loc, s_ctx = scores[g]
        m = jnp.maximum(jnp.max(s_loc, axis=-1, keepdims=True), jnp.max(s_ctx, axis=-1, keepdims=True))
        if sink is not None:
            m = jnp.maximum(m, sink[g])
        acc = (_dot(jnp.exp2(s_loc - m).astype(BF16), vals_ref[g, loc, :])
               + _dot(jnp.exp2(s_ctx - m).astype(BF16), vals_ref[g, ctx, :]))
        if sink is not None:
            acc = acc + jnp.where(half == g, 0.0, jnp.exp2(sink[g] - m))
        outs.append(acc / jnp.where(half == g, pltpu.roll(acc, HEAD_DIM, 1), 1.0))
    return jnp.where(half == 0, outs[0], outs[1])


def _staging_scratch():
    n = DEC_SEQ + PAST_LEN
    return [pltpu.VMEM((n, LANES), BF16), pltpu.VMEM((2, n, LANES), BF16)]


def _lat_blocks(col):
    qt = DEC_SEQ // Q_TILE
    first_q = N_CTX_TOK // Q_TILE
    first_k = N_CTX_TOK // DEC_SEQ
    q = pl.BlockSpec((Q_TILE, LANES), lambda b, j, t: (first_q + qt * b + t, col[0] // LANES + j))
    k = pl.BlockSpec((DEC_SEQ, LANES), lambda b, j, t: (first_k + b, col[1] // LANES + j))
    v = pl.BlockSpec((DEC_SEQ, LANES), lambda b, j, t: (first_k + b, col[2] // LANES + j))
    o = pl.BlockSpec((Q_TILE, LANES), lambda b, j, t: (qt * b + t, j))
    return q, k, v, o


def _cache_spec(layer, which, shared_kv):
    return pl.BlockSpec((None, None, None, PAST_LEN, LANES),
                        lambda b, j, t: (b, layer, which, 0, 0 if shared_kv else j))


def _lat_na(p, cache, bias_rows, layer):
    q, k, v, o = _lat_blocks((NA_Q, NA_K, NA_V))
    rows_spec = pl.BlockSpec((2, N_PAIR_BLOCKS, LANES), lambda b, j, t: (j, 0, 0))
    return pl.pallas_call(
        _lat_na_kernel,
        grid=(DEC_BATCH, 2, DEC_SEQ // Q_TILE),
        in_specs=[q, k, v, _cache_spec(layer, 0, False), _cache_spec(layer, 1, False), rows_spec],
        out_specs=o,
        out_shape=jax.ShapeDtypeStruct((N_LAT_TOK, 256), BF16),
        scratch_shapes=[pltpu.VMEM((2, N_PAIR_BLOCKS, GRID_W, LANES), F32)] + _staging_scratch(),
        compiler_params=_params(("arbitrary",) * 3, VMEM_LIMIT),
        name="lat_na",
    )(p, p, p, cache, cache, bias_rows)


@functools.lru_cache(maxsize=None)
def _rope_tables(dim):
    quarter = dim // 4
    pos = np.arange(DEC_SEQ)
    rows, cols = pos // GRID_W, pos % GRID_W
    lane = np.arange(LANES)
    w = lane % dim
    axis_pos = np.where((w // (dim // 2) == 0)[None, :], rows[:, None], cols[:, None]).astype(np.float64)
    u = w % (dim // 2)
    inv = ROPE_BASE ** (-(u % quarter).astype(np.float64) * 2.0 / (dim // 2))
    ang = axis_pos * inv[None, :]
    sign = np.where(u < quarter, -1.0, 1.0)[None, :]
    return np.cos(ang).astype(np.float32), (np.sin(ang) * sign).astype(np.float32)


def _swa_key_start(t):
    return jnp.clip(Q_TILE * t - SWA_WINDOW, 0, DEC_SEQ - SWA_KEYS)


def _lat_swa_kernel(q_ref, k_ref, v_ref, ck_ref, cv_ref, cos_ref, sin_ref, sink_ref, o_ref, keys_ref, vals_ref):
    j = pl.program_id(1)
    t = pl.program_id(2)
    quarter = HEAD_DIM // 4

    @pl.when(t == 0)
    def _():
        def head_j(x):
            return jnp.where(_lane(x.shape) // HEAD_DIM == j, x, pltpu.roll(x, HEAD_DIM, 1))

        _stage_keys_values(keys_ref, vals_ref, head_j(_rope(k_ref[...], cos_ref[...], sin_ref[...], quarter)),
                           head_j(v_ref[...]), head_j(ck_ref[...]), head_j(cv_ref[...]))

    q0 = pl.multiple_of(t * Q_TILE, Q_TILE)
    k0 = pl.multiple_of(_swa_key_start(t), SWA_WINDOW)
    q2 = _rope(q_ref[...], cos_ref[pl.ds(q0, Q_TILE), :], sin_ref[pl.ds(q0, Q_TILE), :], quarter)
    q2 = q2 * (HEAD_DIM ** -0.5 * LOG2E)
    qpos = q0 + lax.broadcasted_iota(jnp.int32, (Q_TILE, SWA_KEYS), 0)
    kpos = k0 + lax.broadcasted_iota(jnp.int32, (Q_TILE, SWA_KEYS), 1)
    valid = jnp.abs(kpos - qpos) <= SWA_WINDOW
    sink = [sink_ref[2 * j + g] * LOG2E for g in range(2)]
    out = _attend_pair(q2, keys_ref, vals_ref, k0, SWA_KEYS, lambda g, s: jnp.where(valid, s, NEG_INF), sink)
    o_ref[...] = out.astype(o_ref.dtype)


def _lat_swa(p, cache, sink, layer):
    q, _, _, o = _lat_blocks((SWA_Q, SWA_K, SWA_V))
    first_k = N_CTX_TOK // DEC_SEQ
    k = pl.BlockSpec((DEC_SEQ, LANES), lambda b, j, t: (first_k + b, SWA_K // LANES))
    v = pl.BlockSpec((DEC_SEQ, LANES), lambda b, j, t: (first_k + b, SWA_V // LANES))
    cos, sin = (jnp.asarray(a) for a in _rope_tables(HEAD_DIM))
    tab = pl.BlockSpec((DEC_SEQ, LANES), lambda b, j, t: (0, 0))
    return pl.pallas_call(
        _lat_swa_kernel,
        grid=(DEC_BATCH, 2, DEC_SEQ // Q_TILE),
        in_specs=[q, k, v, _cache_spec(layer, 0, True), _cache_spec(layer, 1, True), tab, tab,
                  pl.BlockSpec(memory_space=pltpu.SMEM)],
        out_specs=o,
        out_shape=jax.ShapeDtypeStruct((N_LAT_TOK, 256), BF16),
        scratch_shapes=_staging_scratch(),
        compiler_params=_params(("arbitrary",) * 3, VMEM_LIMIT),
        name="lat_swa",
    )(p, p, p, cache, cache, cos, sin, sink)


DIFF_KEYS = DEC_SEQ + PAST_LEN
KEY_CHUNK = 256


def _lat_diff_kernel(q_ref, k_ref, v_ref, ck_ref, cv_ref, cos_ref, sin_ref, lp_ref, sg_ref, o_ref,
                     keys_ref, vals_ref, s_ref, *, lam_init):
    t = pl.program_id(2)
    eighth = DIFF_QK_DIM // 4

    @pl.when(t == 0)
    def _():
        _stage_keys_values(keys_ref, vals_ref, _rope(k_ref[...], cos_ref[...], sin_ref[...], eighth), v_ref[...],
                           ck_ref[...], cv_ref[...])

    q0 = pl.multiple_of(t * Q_TILE, Q_TILE)
    q2 = _rope(q_ref[...], cos_ref[pl.ds(q0, Q_TILE), :], sin_ref[pl.ds(q0, Q_TILE), :], eighth)
    q2 = q2 * (DIFF_QK_DIM ** -0.5 * LOG2E)
    lam = _lambda(lp_ref[...], lam_init)
    lane = _lane(q2.shape)
    quarter = lane // DIFF_QK_DIM
    half = lane // HEAD_DIM
    n_chunks = DIFF_KEYS // KEY_CHUNK

    def scores(i):
        qm = jnp.where(quarter == i, q2, 0.0).astype(BF16)
        m_run = None
        for n in range(n_chunks):
            cols = slice(n * KEY_CHUNK, (n + 1) * KEY_CHUNK)
            s = _dot_nt(qm, keys_ref[cols, :])
            s_ref[i, :, cols] = s
            m_blk = jnp.maximum(s[:, :LANES], s[:, LANES:])
            m_run = m_blk if m_run is None else jnp.maximum(m_run, m_blk)
        return jnp.max(m_run, axis=-1, keepdims=True)

    def values(i, m):
        g = i // 2
        acc = None
        for n in range(n_chunks):
            cols = slice(n * KEY_CHUNK, (n + 1) * KEY_CHUNK)
            part = _dot(jnp.exp2(s_ref[i, :, cols] - m).astype(BF16), vals_ref[g, cols, :])
            acc = part if acc is None else acc + part
        return acc / jnp.where(half == g, pltpu.roll(acc, HEAD_DIM, 1), 1.0)

    o = []
    m_next = scores(0)
    for i in range(4):
        m = m_next
        if i + 1 < 4:
            m_next = scores(i + 1)
        o.append(values(i, m))
    outs = [o[0] - lam * o[1], o[2] - lam * o[3]]
    o2 = jnp.where(half == 0, outs[0], outs[1])
    o_ref[...] = _subln(o2, sg_ref[...], lam_init).astype(o_ref.dtype)


def _lat_diff(p, cache, lp, sg2, layer, lam_init):
    q, k, v, o = _lat_blocks((DIFF_Q, DIFF_K, DIFF_V))
    cos, sin = (jnp.asarray(a) for a in _rope_tables(DIFF_QK_DIM))
    tab = pl.BlockSpec((DEC_SEQ, LANES), lambda b, j, t: (0, 0))
    return pl.pallas_call(
        functools.partial(_lat_diff_kernel, lam_init=lam_init),
        grid=(DEC_BATCH, 2, DEC_SEQ // Q_TILE),
        in_specs=[q, k, v, _cache_spec(layer, 0, False), _cache_spec(layer, 1, False), tab, tab,
                  pl.BlockSpec((4, DIFF_QK_DIM), lambda b, j, t: (0, 0)),
                  pl.BlockSpec((1, LANES), lambda b, j, t: (0, 0))],
        out_specs=o,
        out_shape=jax.ShapeDtypeStruct((N_LAT_TOK, 256), BF16),
        scratch_shapes=_staging_scratch() + [pltpu.VMEM((4, Q_TILE, DIFF_KEYS), F32)],
        compiler_params=_params(("arbitrary",) * 3, VMEM_LIMIT),
        name="lat_diff",
    )(p, p, p, cache, cache, cos, sin, lp, sg2)


def kernel(x_prompt, x_sample, cache_na_kv, cache_swa_kv, cache_diff_kv, c, c_ctx, norm1_g, norm2_g, ada_w,
           ada_b, w_in, na_rpb, swa_sink, diff_lambda, diff_subln_g, w_branch, w_gate, b_gate, w_o, w_ffn_in,
           w_ffn_out, final_norm_g):
    x = (x_prompt.reshape(N_CTX_TOK, D_MODEL), x_sample.reshape(N_LAT_TOK, D_MODEL))
    cond = jnp.zeros((COND_ROWS, D_MODEL), F32).at[0].set(c_ctx).at[1:1 + DEC_BATCH].set(c)
    mod = _adaln(cond, ada_w, ada_b).reshape(DEPTH, COND_ROWS, 6, D_MODEL)

    cache_na = cache_na_kv.reshape(DEC_BATCH, DEPTH, 2, PAST_LEN, 256)
    cache_swa = cache_swa_kv.reshape(DEC_BATCH, DEPTH, 2, PAST_LEN, LANES)
    cache_diff = cache_diff_kv.reshape(DEC_BATCH, DEPTH, 2, PAST_LEN, 256)
    fg = final_norm_g.reshape(1, D_MODEL)

    states = None
    for l in range(DEPTH):
        lam_init = 0.8 - 0.6 * math.exp(-0.3 * l)
        g1 = norm1_g[l].reshape(1, D_MODEL)
        sg2 = jnp.tile(diff_subln_g[l], 2).reshape(1, LANES)
        p, h, *states = _proj(x, mod, l, g1, w_in[l].astype(BF16), states)

        c_na, c_swa, c_f, c_diff = _ctx_mixers(p, swa_sink[l], diff_lambda[l], sg2, lam_init)
        l_na = _lat_na(p, cache_na, _na_bias_rows(na_rpb[l]), l)
        l_swa = _lat_swa(p, cache_swa, swa_sink[l], l)
        l_f = _fourier_latent(p)
        l_diff = _lat_diff(p, cache_diff, diff_lambda[l], sg2, l, lam_init)

        x = _merge(x, mod, l, h, (c_na, c_swa, c_f, c_diff), (l_na, l_swa, l_f, l_diff),
                   w_gate[l].astype(BF16), b_gate[l].reshape(1, -1), w_branch[l].astype(BF16),
                   w_o[l].astype(BF16))
        x = _ffn(x, mod, l, norm2_g[l].reshape(1, D_MODEL), w_ffn_in[l].astype(BF16),
                 w_ffn_out[l].astype(BF16), fg, final=(l == DEPTH - 1))

    y_ctx, y_lat = x
    na_kv, swa_kv, diff_kv = states
    return (y_ctx.reshape(BATCH, SEQ, D_MODEL), y_lat.reshape(DEC_BATCH, DEC_SEQ, D_MODEL),
            na_kv.reshape(BATCH, DEPTH, 2, SEQ, 4, HEAD_DIM), swa_kv.reshape(BATCH, DEPTH, 2, SEQ, 2, HEAD_DIM),
            diff_kv.reshape(BATCH, DEPTH, 2, SEQ, 4, HEAD_DIM))
```

```python
import functools
import math

import numpy as np
import jax
import jax.numpy as jnp
from jax import lax
from jax.experimental import pallas as pl
from jax.experimental.pallas import tpu as pltpu

D_MODEL = 1024
BATCH = 16
SEQ = 256
DEPTH = 2
DEC_BATCH = 4
DEC_SEQ = 2048
PAST_LEN = 512
GRID_W = 64
GRID_H = DEC_SEQ // GRID_W
HEAD_DIM = 64
NA_WIN_H = 8
NA_WIN_W = 16
SWA_WINDOW = 128
DIFF_QK_DIM = 32
D_FF = 2816
D_IN = 2304
ROPE_BASE = 10000.0
NORM_EPS = 1e-6
NEG_INF = -1e30
LOG2E = math.log2(math.e)

NA_Q, NA_K, NA_V = 0, 256, 512
SWA_Q, SWA_K, SWA_V = 768, 1024, 1152
FNET_U = 1280
DIFF_Q, DIFF_K, DIFF_V = 1536, 1792, 2048

LANES = 128
N_CTX_TOK = BATCH * SEQ
N_LAT_TOK = DEC_BATCH * DEC_SEQ
N_TOK = N_CTX_TOK + N_LAT_TOK
COND_ROWS = 8
TOKEN_TILE = 512
Q_TILE = 256
NA_TILE_ROWS = Q_TILE // GRID_W
NA_BAND_ROWS = NA_WIN_H + NA_TILE_ROWS
SWA_KEYS = 2 * Q_TILE
VMEM_LIMIT = 56 * 1024 * 1024

F32 = jnp.float32
BF16 = jnp.bfloat16


def _params(semantics, vmem=None):
    return pltpu.CompilerParams(dimension_semantics=semantics, vmem_limit_bytes=vmem)


def _dot(a, b):
    return jnp.dot(a, b, preferred_element_type=F32)


def _dot_nt(a, b):
    return lax.dot_general(a, b, (((1,), (1,)), ((), ())), preferred_element_type=F32)


def _split(x):
    hi = x.astype(BF16)
    lo = (x - hi.astype(F32)).astype(BF16)
    return hi, lo


def _dot3(a_hi, a_lo, b_hi, b_lo):
    return _dot(a_hi, b_hi) + _dot(a_lo, b_hi) + _dot(a_hi, b_lo)


def _sigmoid(x):
    return 1.0 / (1.0 + jnp.exp(-x))


def _rms(x, g):
    return x * lax.rsqrt(jnp.mean(x * x, axis=-1, keepdims=True) + NORM_EPS) * g


def _exp_parts(blocks, extra=None):
    m = None
    for s in blocks:
        mi = jnp.max(s, axis=-1, keepdims=True)
        m = mi if m is None else jnp.maximum(m, mi)
    if extra is not None:
        m = jnp.maximum(m, extra)
    es = [jnp.exp2(s - m) for s in blocks]
    l = None
    for e in es:
        li = jnp.sum(e, axis=-1, keepdims=True)
        l = li if l is None else l + li
    if extra is not None:
        l = l + jnp.exp2(extra - m)
    return es, l


def _lane(shape):
    return lax.broadcasted_iota(jnp.int32, shape, 1)


def _rope(x, cos, sin_signed, half):
    lane = _lane(x.shape)
    partner = jnp.where((lane % (2 * half)) < half,
                        pltpu.roll(x, LANES - half, 1), pltpu.roll(x, half, 1))
    return x * cos + partner * sin_signed


def _lambda(lp, lam_init):
    s1 = jnp.sum(lp[0:1, :] * lp[1:2, :], axis=-1, keepdims=True)
    s2 = jnp.sum(lp[2:3, :] * lp[3:4, :], axis=-1, keepdims=True)
    return jnp.exp(s1) - jnp.exp(s2) + lam_init


def _subln(o, g2, lam_init):
    lane = _lane(o.shape)
    sq = o * o
    ms0 = jnp.sum(jnp.where(lane < HEAD_DIM, sq, 0.0), axis=-1, keepdims=True)
    ms1 = jnp.sum(jnp.where(lane >= HEAD_DIM, sq, 0.0), axis=-1, keepdims=True)
    ms = jnp.where(lane < HEAD_DIM, ms0, ms1) * (1.0 / HEAD_DIM)
    return (o * lax.rsqrt(ms + NORM_EPS) * g2) * (1.0 - lam_init)


def _adaln_kernel(cond_ref, w_ref, b_ref, o_ref):
    c = cond_ref[...]
    s = c * _sigmoid(c)
    s_hi, s_lo = _split(s)
    w_hi, w_lo = _split(w_ref[...])
    o_ref[...] = _dot3(s_hi, s_lo, w_hi, w_lo) + b_ref[...]


def _adaln(cond, ada_w, ada_b):
    tn = 1536
    n = 6 * D_MODEL
    return pl.pallas_call(
        _adaln_kernel,
        grid=(DEPTH, n // tn),
        in_specs=[
            pl.BlockSpec((COND_ROWS, D_MODEL), lambda l, j: (0, 0)),
            pl.BlockSpec((None, D_MODEL, tn), lambda l, j: (l, 0, j)),
            pl.BlockSpec((None, 1, tn), lambda l, j: (l, 0, j)),
        ],
        out_specs=pl.BlockSpec((None, COND_ROWS, tn), lambda l, j: (l, 0, j)),
        out_shape=jax.ShapeDtypeStruct((DEPTH, COND_ROWS, n), F32),
        compiler_params=_params(("arbitrary", "arbitrary")),
        name="adaln",
    )(cond, ada_w, ada_b.reshape(DEPTH, 1, n))


def _cond_row(i):
    n_ctx = N_CTX_TOK // TOKEN_TILE
    per_seq = DEC_SEQ // TOKEN_TILE
    return jnp.where(i < n_ctx, 0, 1 + (i - n_ctx) // per_seq)


def _mod_spec(layer):
    return pl.BlockSpec((None, None, 6, D_MODEL), lambda i: (layer, _cond_row(i), 0, 0))


ROW_SPLIT = 2
N_CTX_TILES = N_CTX_TOK // TOKEN_TILE
N_LAT_TILES = N_LAT_TOK // TOKEN_TILE


def _tok_spec(width):
    return pl.BlockSpec((TOKEN_TILE, width), lambda i: (i, 0))


def _ctx_tile_spec(width):
    return pl.BlockSpec((TOKEN_TILE, width), lambda i: (jnp.minimum(i, N_CTX_TILES - 1), 0))


def _lat_tile_spec(width):
    return pl.BlockSpec((TOKEN_TILE, width), lambda i: (jnp.clip(i - N_CTX_TILES, 0, N_LAT_TILES - 1), 0))


def _const_spec(shape):
    nd = len(shape)
    return pl.BlockSpec(shape, lambda i: (0,) * nd, pipeline_mode=pl.Buffered(1))


def _is_ctx_tile():
    return pl.program_id(0) < N_CTX_TILES


def _x_specs(x):
    return [_ctx_tile_spec(D_MODEL), _lat_tile_spec(D_MODEL)] if isinstance(x, tuple) else [_tok_spec(D_MODEL)]


def _x_args(x):
    return list(x) if isinstance(x, tuple) else [x]


def _load_x(x_refs, rows=slice(None)):
    if len(x_refs) == 1:
        return x_refs[0][rows, :]
    return jnp.where(_is_ctx_tile(), x_refs[0][rows, :], x_refs[1][rows, :])


def _proj_kernel(*refs, n_x):
    x_refs = refs[:n_x]
    mod_ref, g_ref, w_ref = refs[n_x:n_x + 3]
    p_ref, h_ref, na_ref, swa_ref, diff_ref = refs[-5:]
    for r in range(ROW_SPLIT):
        rows = slice(r * TOKEN_TILE // ROW_SPLIT, (r + 1) * TOKEN_TILE // ROW_SPLIT)
        h = _rms(_load_x(x_refs, rows), g_ref[...]) * (1.0 + mod_ref[1:2, :]) + mod_ref[0:1, :]
        hb = h.astype(BF16)
        h_ref[rows, :] = hb
        p_ref[rows, :] = _dot(hb, w_ref[...])

    @pl.when(_is_ctx_tile())
    def _():
        for s in range(TOKEN_TILE // SEQ):
            rows = slice(s * SEQ, (s + 1) * SEQ)
            for which in range(2):
                na_ref[s, which] = p_ref[rows, NA_K + 256 * which: NA_K + 256 * (which + 1)]
                swa_ref[s, which] = p_ref[rows, SWA_K + LANES * which: SWA_K + LANES * (which + 1)]
                diff_ref[s, which] = p_ref[rows, DIFF_K + 256 * which: DIFF_K + 256 * (which + 1)]


STATE_WIDTHS = (256, LANES, 256)


def _proj(x, mod, layer, g, w, states):
    seqs = TOKEN_TILE // SEQ

    def state_spec(width):
        return pl.BlockSpec((seqs, None, 2, SEQ, width),
                            lambda i: (jnp.minimum(i, N_CTX_TILES - 1), layer, 0, 0, 0))

    n_x = len(_x_args(x))
    n_in = n_x + 3
    prior = [] if states is None else list(states)
    return pl.pallas_call(
        functools.partial(_proj_kernel, n_x=n_x),
        grid=(N_TOK // TOKEN_TILE,),
        in_specs=_x_specs(x) + [_mod_spec(layer), _const_spec((1, D_MODEL)), _const_spec((D_MODEL, D_IN))]
                 + [pl.BlockSpec(memory_space=pl.ANY)] * len(prior),
        out_specs=[_tok_spec(D_IN), _tok_spec(D_MODEL)] + [state_spec(w_) for w_ in STATE_WIDTHS],
        out_shape=[jax.ShapeDtypeStruct((N_TOK, D_IN), F32), jax.ShapeDtypeStruct((N_TOK, D_MODEL), BF16)]
                  + [jax.ShapeDtypeStruct((BATCH, DEPTH, 2, SEQ, w_), F32) for w_ in STATE_WIDTHS],
        input_output_aliases={n_in + k: 2 + k for k in range(len(prior))},
        compiler_params=_params(("arbitrary",), VMEM_LIMIT),
        name="proj",
    )(*_x_args(x), mod, g, w, *prior)


N_BRANCH = 4
BRANCH_DIM = D_MODEL // N_BRANCH


def _merge_kernel(*refs, n_x):
    x_refs = refs[:n_x]
    mod_ref, h_ref = refs[n_x:n_x + 2]
    ctx_refs = refs[n_x + 2:n_x + 2 + N_BRANCH]
    lat_refs = refs[n_x + 2 + N_BRANCH:n_x + 2 + 2 * N_BRANCH]
    wg_ref, bg_ref, wb_ref, wo_ref, o_ref = refs[n_x + 2 + 2 * N_BRANCH:]
    is_ctx = _is_ctx_tile()
    for r in range(ROW_SPLIT):
        rows = slice(r * TOKEN_TILE // ROW_SPLIT, (r + 1) * TOKEN_TILE // ROW_SPLIT)
        hb = h_ref[rows, :]
        merged = None
        for k in range(N_BRANCH):
            cols = slice(k * D_MODEL, (k + 1) * D_MODEL)
            gate = _sigmoid(_dot(hb, wg_ref[:, cols]) + bg_ref[:, cols])
            branch = jnp.where(is_ctx, ctx_refs[k][rows, :], lat_refs[k][rows, :])
            term = gate * _dot(branch, wb_ref[k])
            merged = term if merged is None else merged + term
        o_ref[rows, :] = _load_x(x_refs, rows) + mod_ref[2:3, :] * _dot(merged.astype(BF16), wo_ref[...])


def _merge(x, mod, layer, h, ctx_branches, lat_branches, wg, bg, wb, wo):
    return pl.pallas_call(
        functools.partial(_merge_kernel, n_x=len(_x_args(x))),
        grid=(N_TOK // TOKEN_TILE,),
        in_specs=_x_specs(x) + [_mod_spec(layer), _tok_spec(D_MODEL)]
                 + [_ctx_tile_spec(BRANCH_DIM)] * N_BRANCH + [_lat_tile_spec(BRANCH_DIM)] * N_BRANCH
                 + [_const_spec((D_MODEL, N_BRANCH * D_MODEL)), _const_spec((1, N_BRANCH * D_MODEL)),
                    _const_spec((N_BRANCH, BRANCH_DIM, D_MODEL)), _const_spec((D_MODEL, D_MODEL))],
        out_specs=_tok_spec(D_MODEL),
        out_shape=jax.ShapeDtypeStruct((N_TOK, D_MODEL), F32),
        compiler_params=_params(("arbitrary",), VMEM_LIMIT),
        name="merge",
    )(*_x_args(x), mod, h, *ctx_branches, *lat_branches, wg, bg, wb, wo)


def _ffn_kernel(x_ref, mod_ref, g_ref, wi_ref, wo_ref, fg_ref, *o_refs, final):
    ys = []
    for r in range(ROW_SPLIT):
        rows = slice(r * TOKEN_TILE // ROW_SPLIT, (r + 1) * TOKEN_TILE // ROW_SPLIT)
        x = x_ref[rows, :]
        h = _rms(x, g_ref[...]) * (1.0 + mod_ref[4:5, :]) + mod_ref[3:4, :]
        hb = h.astype(BF16)
        a = _dot(hb, wi_ref[:, :D_FF])
        b = _dot(hb, wi_ref[:, D_FF:])
        f = (a * _sigmoid(a)) * b
        y = x + mod_ref[5:6, :] * _dot(f.astype(BF16), wo_ref[...])
        if not final:
            o_refs[0][rows, :] = y
        else:
            ys.append(_rms(y, fg_ref[...]))
    if not final:
        return
    y = jnp.concatenate(ys, axis=0)
    ctx_ref, lat_ref = o_refs

    @pl.when(_is_ctx_tile())
    def _():
        ctx_ref[...] = y

    @pl.when(jnp.logical_not(_is_ctx_tile()))
    def _():
        lat_ref[...] = y


def _ffn(x, mod, layer, g, wi, wo, fg, final):
    if final:
        out_specs = [_ctx_tile_spec(D_MODEL), _lat_tile_spec(D_MODEL)]
        out_shape = [jax.ShapeDtypeStruct((N_CTX_TOK, D_MODEL), F32),
                     jax.ShapeDtypeStruct((N_LAT_TOK, D_MODEL), F32)]
    else:
        out_specs = _tok_spec(D_MODEL)
        out_shape = jax.ShapeDtypeStruct((N_TOK, D_MODEL), F32)
    return pl.pallas_call(
        functools.partial(_ffn_kernel, final=final),
        grid=(N_TOK // TOKEN_TILE,),
        in_specs=[_tok_spec(D_MODEL), _mod_spec(layer), _const_spec((1, D_MODEL)),
                  _const_spec((D_MODEL, 2 * D_FF)), _const_spec((D_FF, D_MODEL)),
                  _const_spec((1, D_MODEL))],
        out_specs=out_specs,
        out_shape=out_shape,
        compiler_params=_params(("arbitrary",), VMEM_LIMIT),
        name="ffn",
    )(x, mod, g, wi, wo, fg)


@functools.lru_cache(maxsize=None)
def _dft_tables(n, blocks):
    k = np.arange(n, dtype=np.int64)
    ang = 2.0 * np.pi * ((k[:, None] * k[None, :]) % n).astype(np.float64) / n
    out = []
    for m in (np.cos(ang), np.sin(ang)):
        m = np.kron(np.eye(blocks), m / math.sqrt(n)).astype(np.float32)
        hi = m.astype(BF16)
        lo = (m - hi.astype(np.float32)).astype(BF16)
        out += [hi, lo]
    return tuple(out)


def _fourier_direct(u, pos_refs, chan_refs):
    ch_ref, cl_ref, sh_ref, sl_ref = pos_refs
    cch_ref, ccl_ref, sch_ref, scl_ref = chan_refs
    u_hi, u_lo = _split(u)
    a_hi, a_lo = _split(_dot3(ch_ref[...], cl_ref[...], u_hi, u_lo))
    b_hi, b_lo = _split(_dot3(sh_ref[...], sl_ref[...], u_hi, u_lo))
    return _dot3(a_hi, a_lo, cch_ref[...], ccl_ref[...]) - _dot3(b_hi, b_lo, sch_ref[...], scl_ref[...])


FFT_RADIX = 8
FFT_INNER = DEC_SEQ // FFT_RADIX


@functools.lru_cache(maxsize=None)
def _twiddle_tables():
    r = np.arange(FFT_RADIX, dtype=np.float64)[:, None]
    f2 = np.arange(FFT_INNER, dtype=np.float64)[None, :]
    ang = 2.0 * np.pi * r * f2 / DEC_SEQ
    scale = 1.0 / math.sqrt(FFT_RADIX)
    shape = (FFT_RADIX, FFT_INNER, 256)
    return tuple(np.ascontiguousarray(np.broadcast_to((f(ang) * scale)[:, :, None], shape)).astype(np.float32)
                 for f in (np.cos, np.sin))


def _fourier_latent_kernel(u0_ref, u1_ref, ch_ref, cl_ref, sh_ref, sl_ref, twc_ref, tws_ref, cch_ref, ccl_ref,
                           sch_ref, scl_ref, o_ref, tre_ref, tim_ref):
    for r in range(FFT_RADIX):
        rows = pl.ds(r, FFT_INNER, stride=FFT_RADIX)
        x_hi, x_lo = _split(jnp.concatenate([u0_ref[rows, :], u1_ref[rows, :]], axis=1))
        g_re = _dot3(ch_ref[...], cl_ref[...], x_hi, x_lo)
        g_im = -_dot3(sh_ref[...], sl_ref[...], x_hi, x_lo)
        wc, ws = twc_ref[r], tws_ref[r]
        tre_ref[r] = g_re * wc + g_im * ws
        tim_ref[r] = g_im * wc - g_re * ws

    def axpy(acc, coef, x):
        if abs(coef) < 1e-9:
            return acc
        term = x if abs(coef - 1.0) < 1e-9 else (-x if abs(coef + 1.0) < 1e-9 else coef * x)
        return term if acc is None else acc + term

    for f1 in range(FFT_RADIX):
        p_re = p_im = None
        for r in range(FFT_RADIX):
            ang = 2.0 * math.pi * ((f1 * r) % FFT_RADIX) / FFT_RADIX
            a, b = math.cos(ang), math.sin(ang)
            t_re, t_im = tre_ref[r], tim_ref[r]
            p_re = axpy(axpy(p_re, a, t_re), b, t_im)
            p_im = axpy(axpy(p_im, a, t_im), -b, t_re)
        re_hi, re_lo = _split(p_re)
        im_hi, im_lo = _split(p_im)
        y = _dot3(re_hi, re_lo, cch_ref[...], ccl_ref[...]) + _dot3(im_hi, im_lo, sch_ref[...], scl_ref[...])
        o_ref[f1 * FFT_INNER:(f1 + 1) * FFT_INNER, :] = y.astype(o_ref.dtype)


def _fourier_latent(p):
    width = 256
    first_blk = N_CTX_TOK // DEC_SEQ
    inner = [jnp.asarray(t) for t in _dft_tables(FFT_INNER, 1)]
    chan = [jnp.asarray(t) for t in _dft_tables(width // 4, 4)]
    tw = [jnp.asarray(t) for t in _twiddle_tables()]

    def u_spec(half):
        return pl.BlockSpec((DEC_SEQ, LANES), lambda b: (first_blk + b, FNET_U // LANES + half))

    def table(shape):
        nd = len(shape)
        return pl.BlockSpec(shape, lambda b: (0,) * nd, pipeline_mode=pl.Buffered(1))

    return pl.pallas_call(
        _fourier_latent_kernel,
        grid=(DEC_BATCH,),
        in_specs=[u_spec(0), u_spec(1)] + [table((FFT_INNER, FFT_INNER))] * 4
                 + [table((FFT_RADIX, FFT_INNER, width))] * 2 + [table((width, width))] * 4,
        out_specs=pl.BlockSpec((DEC_SEQ, width), lambda b: (b, 0)),
        out_shape=jax.ShapeDtypeStruct((N_LAT_TOK, width), BF16),
        scratch_shapes=[pltpu.VMEM((FFT_RADIX, FFT_INNER, width), F32)] * 2,
        compiler_params=_params(("arbitrary",), VMEM_LIMIT),
        name="fourier_lat",
    )(p, p, *inner, *tw, *chan)


def _ctx_mixer_kernel(p_ref, sink_ref, lp_ref, sg_ref, *refs, lam_init):
    pos_refs, chan_refs = refs[:4], refs[4:8]
    na_ref, swa_ref, fnet_ref, diff_ref = refs[8:]
    fnet_ref[...] = _fourier_direct(p_ref[:, FNET_U:FNET_U + 256], pos_refs, chan_refs).astype(fnet_ref.dtype)

    shape = (SEQ, LANES)
    lane = _lane(shape)
    half = lane // HEAD_DIM
    quarter = lane // DIFF_QK_DIM
    scale = HEAD_DIM ** -0.5 * LOG2E

    def pair(col, j):
        return p_ref[:, col + LANES * j: col + LANES * (j + 1)]

    for j in range(2):
        q2 = pair(NA_Q, j) * scale
        kb = pair(NA_K, j).astype(BF16)
        vb = pair(NA_V, j).astype(BF16)
        outs = []
        for g in range(2):
            qm = jnp.where(half == g, q2, 0.0).astype(BF16)
            (e,), l = _exp_parts([_dot_nt(qm, kb)])
            outs.append(_dot(e.astype(BF16), vb) / l)
        na_ref[:, LANES * j: LANES * (j + 1)] = jnp.where(half == 0, outs[0], outs[1]).astype(na_ref.dtype)

    k2 = p_ref[:, SWA_K: SWA_K + LANES]
    v2 = p_ref[:, SWA_V: SWA_V + LANES]
    k2s = pltpu.roll(k2, HEAD_DIM, 1)
    v2s = pltpu.roll(v2, HEAD_DIM, 1)
    for j in range(2):
        kb = jnp.where(half == j, k2, k2s).astype(BF16)
        vb = jnp.where(half == j, v2, v2s).astype(BF16)
        q2 = pair(SWA_Q, j) * scale
        outs = []
        for g in range(2):
            qm = jnp.where(half == g, q2, 0.0).astype(BF16)
            (e,), l = _exp_parts([_dot_nt(qm, kb)], extra=sink_ref[2 * j + g] * LOG2E)
            outs.append(_dot(e.astype(BF16), vb) / l)
        swa_ref[:, LANES * j: LANES * (j + 1)] = jnp.where(half == 0, outs[0], outs[1]).astype(swa_ref.dtype)

    lam = _lambda(lp_ref[...], lam_init)
    for j in range(2):
        q2 = pair(DIFF_Q, j) * (DIFF_QK_DIM ** -0.5 * LOG2E)
        kb = pair(DIFF_K, j).astype(BF16)
        vb = pair(DIFF_V, j).astype(BF16)
        outs = []
        for g in range(2):
            o = []
            for c in range(2):
                qm = jnp.where(quarter == 2 * g + c, q2, 0.0).astype(BF16)
                (e,), l = _exp_parts([_dot_nt(qm, kb)])
                o.append(_dot(e.astype(BF16), vb) / l)
            outs.append(o[0] - lam * o[1])
        o2 = jnp.where(half == 0, outs[0], outs[1])
        diff_ref[:, LANES * j: LANES * (j + 1)] = _subln(o2, sg_ref[...], lam_init).astype(diff_ref.dtype)


def _ctx_mixers(p, sink, lp, sg2, lam_init):
    out = jax.ShapeDtypeStruct((N_CTX_TOK, 256), BF16)
    ospec = pl.BlockSpec((SEQ, 256), lambda b: (b, 0))
    tables = [jnp.asarray(t) for t in _dft_tables(SEQ, 1) + _dft_tables(256 // 4, 4)]
    table_spec = pl.BlockSpec((256, 256), lambda b: (0, 0), pipeline_mode=pl.Buffered(1))
    return pl.pallas_call(
        functools.partial(_ctx_mixer_kernel, lam_init=lam_init),
        grid=(BATCH,),
        in_specs=[pl.BlockSpec((SEQ, D_IN), lambda b: (b, 0)),
                  pl.BlockSpec(memory_space=pltpu.SMEM),
                  pl.BlockSpec((4, DIFF_QK_DIM), lambda b: (0, 0)),
                  pl.BlockSpec((1, LANES), lambda b: (0, 0))] + [table_spec] * 8,
        out_specs=[ospec] * 4,
        out_shape=[out] * 4,
        compiler_params=_params(("arbitrary",), VMEM_LIMIT),
        name="ctx_mixers",
    )(p, sink, lp, sg2, *tables)


N_DR = 2 * NA_WIN_H - 1
N_DC = 2 * NA_WIN_W - 1
N_PAIR_BLOCKS = N_DR + 1


def _na_band_start(t):
    return jnp.clip(NA_TILE_ROWS * t - NA_WIN_H // 2, 0, GRID_H - NA_BAND_ROWS)


def _na_bias_rows(rpb):
    rows = jnp.pad(rpb, ((0, 0), (1, 1), (0, GRID_W - N_DC)))
    return jnp.concatenate([rows[:, :N_PAIR_BLOCKS], rows[:, 1:]], axis=-1)


def _lat_na_kernel(q_ref, k_ref, v_ref, ck_ref, cv_ref, rows_ref, o_ref, pair_ref, keys_ref, vals_ref):
    t = pl.program_id(2)
    blk = (GRID_W, LANES)

    @pl.when(t == 0)
    def _():
        c = lax.broadcasted_iota(jnp.int32, blk, 0)
        kc = _lane(blk) % GRID_W
        cs = jnp.clip(c - NA_WIN_W // 2, 0, GRID_W - NA_WIN_W)
        in_cols = (kc >= cs) & (kc < cs + NA_WIN_W)
        for g in range(2):
            for i in range(N_PAIR_BLOCKS):
                row = jnp.broadcast_to(rows_ref[g, i:i + 1, :], blk) * LOG2E
                toeplitz = pltpu.roll(row, LANES - (NA_WIN_W - 1), 1, stride=1, stride_axis=0)
                pair_ref[g, i] = jnp.where(in_cols, toeplitz, NEG_INF)

        _stage_keys_values(keys_ref, vals_ref, k_ref[...], v_ref[...], ck_ref[...], cv_ref[...])

    band = _na_band_start(t)
    start = pl.multiple_of(band * GRID_W, GRID_W)
    nk = NA_BAND_ROWS * GRID_W
    q2 = q_ref[...] * (HEAD_DIM ** -0.5 * LOG2E)

    def bias(g):
        rows = []
        for ri in range(NA_TILE_ROWS):
            r = NA_TILE_ROWS * t + ri
            first = jnp.clip(r - NA_WIN_H // 2, 0, GRID_H - NA_WIN_H) - band
            d0 = band - r + NA_WIN_H - 1
            blocks = []
            for m in range(NA_BAND_ROWS // 2):
                idx = jnp.clip(d0 + 2 * m + 1, 0, N_PAIR_BLOCKS - 1)
                jrow = 2 * m + _lane(blk) // GRID_W
                in_rows = (jrow >= first) & (jrow < first + NA_WIN_H)
                blocks.append(jnp.where(in_rows, pair_ref[g, idx], NEG_INF))
            rows.append(jnp.concatenate(blocks, axis=1))
        return jnp.concatenate(rows, axis=0)

    o_ref[...] = _attend_pair(q2, keys_ref, vals_ref, start, nk, lambda g, s: s + bias(g)).astype(o_ref.dtype)


def _stage_keys_values(keys_ref, vals_ref, k2, v2, ck2, cv2):
    keys_ref[:DEC_SEQ, :] = k2.astype(BF16)
    keys_ref[DEC_SEQ:, :] = ck2.astype(BF16)
    for g in range(2):
        vals_ref[g, :DEC_SEQ, :] = jnp.where(_lane(v2.shape) // HEAD_DIM == g, v2, 1.0).astype(BF16)
        vals_ref[g, DEC_SEQ:, :] = jnp.where(_lane(cv2.shape) // HEAD_DIM == g, cv2, 1.0).astype(BF16)


def _attend_pair(q2, keys_ref, vals_ref, start, n_loc, fix_local, sink=None):
    half = _lane(q2.shape) // HEAD_DIM
    loc = pl.ds(start, n_loc)
    ctx = slice(DEC_SEQ, DEC_SEQ + PAST_LEN)
    scores = []
    for g in range(2):
        qm = jnp.where(half == g, q2, 0.0).astype(BF16)
        scores.append((fix_local(g, _dot_nt(qm, keys_ref[loc, :])), _dot_nt(qm, keys_ref[ctx, :])))
    outs = []
    for g in range(2):
        s_loc, s_ctx = scores[g]
        m = jnp.maximum(jnp.max(s_loc, axis=-1, keepdims=True), jnp.max(s_ctx, axis=-1, keepdims=True))
        if sink is not None:
            m = jnp.maximum(m, sink[g])
        acc = (_dot(jnp.exp2(s_loc - m).astype(BF16), vals_ref[g, loc, :])
               + _dot(jnp.exp2(s_ctx - m).astype(BF16), vals_ref[g, ctx, :]))
        if sink is not None:
            acc = acc + jnp.where(half == g, 0.0, jnp.exp2(sink[g] - m))
        outs.append(acc / jnp.where(half == g, pltpu.roll(acc, HEAD_DIM, 1), 1.0))
    return jnp.where(half == 0, outs[0], outs[1])


def _staging_scratch():
    n = DEC_SEQ + PAST_LEN
    return [pltpu.VMEM((n, LANES), BF16), pltpu.VMEM((2, n, LANES), BF16)]


def _lat_blocks(col):
    qt = DEC_SEQ // Q_TILE
    first_q = N_CTX_TOK // Q_TILE
    first_k = N_CTX_TOK // DEC_SEQ
    q = pl.BlockSpec((Q_TILE, LANES), lambda b, j, t: (first_q + qt * b + t, col[0] // LANES + j))
    k = pl.BlockSpec((DEC_SEQ, LANES), lambda b, j, t: (first_k + b, col[1] // LANES + j))
    v = pl.BlockSpec((DEC_SEQ, LANES), lambda b, j, t: (first_k + b, col[2] // LANES + j))
    o = pl.BlockSpec((Q_TILE, LANES), lambda b, j, t: (qt * b + t, j))
    return q, k, v, o


def _cache_spec(layer, which, shared_kv):
    return pl.BlockSpec((None, None, None, PAST_LEN, LANES),
                        lambda b, j, t: (b, layer, which, 0, 0 if shared_kv else j))


def _lat_na(p, cache, bias_rows, layer):
    q, k, v, o = _lat_blocks((NA_Q, NA_K, NA_V))
    rows_spec = pl.BlockSpec((2, N_PAIR_BLOCKS, LANES), lambda b, j, t: (j, 0, 0))
    return pl.pallas_call(
        _lat_na_kernel,
        grid=(DEC_BATCH, 2, DEC_SEQ // Q_TILE),
        in_specs=[q, k, v, _cache_spec(layer, 0, False), _cache_spec(layer, 1, False), rows_spec],
        out_specs=o,
        out_shape=jax.ShapeDtypeStruct((N_LAT_TOK, 256), BF16),
        scratch_shapes=[pltpu.VMEM((2, N_PAIR_BLOCKS, GRID_W, LANES), F32)] + _staging_scratch(),
        compiler_params=_params(("arbitrary",) * 3, VMEM_LIMIT),
        name="lat_na",
    )(p, p, p, cache, cache, bias_rows)


@functools.lru_cache(maxsize=None)
def _rope_tables(dim):
    quarter = dim // 4
    pos = np.arange(DEC_SEQ)
    rows, cols = pos // GRID_W, pos % GRID_W
    lane = np.arange(LANES)
    w = lane % dim
    axis_pos = np.where((w // (dim // 2) == 0)[None, :], rows[:, None], cols[:, None]).astype(np.float64)
    u = w % (dim // 2)
    inv = ROPE_BASE ** (-(u % quarter).astype(np.float64) * 2.0 / (dim // 2))
    ang = axis_pos * inv[None, :]
    sign = np.where(u < quarter, -1.0, 1.0)[None, :]
    return np.cos(ang).astype(np.float32), (np.sin(ang) * sign).astype(np.float32)


def _swa_key_start(t):
    return jnp.clip(Q_TILE * t - SWA_WINDOW, 0, DEC_SEQ - SWA_KEYS)


def _lat_swa_kernel(q_ref, k_ref, v_ref, ck_ref, cv_ref, cos_ref, sin_ref, sink_ref, o_ref, keys_ref, vals_ref):
    j = pl.program_id(1)
    t = pl.program_id(2)
    quarter = HEAD_DIM // 4

    @pl.when(t == 0)
    def _():
        def head_j(x):
            return jnp.where(_lane(x.shape) // HEAD_DIM == j, x, pltpu.roll(x, HEAD_DIM, 1))

        _stage_keys_values(keys_ref, vals_ref, head_j(_rope(k_ref[...], cos_ref[...], sin_ref[...], quarter)),
                           head_j(v_ref[...]), head_j(ck_ref[...]), head_j(cv_ref[...]))

    q0 = pl.multiple_of(t * Q_TILE, Q_TILE)
    k0 = pl.multiple_of(_swa_key_start(t), SWA_WINDOW)
    q2 = _rope(q_ref[...], cos_ref[pl.ds(q0, Q_TILE), :], sin_ref[pl.ds(q0, Q_TILE), :], quarter)
    q2 = q2 * (HEAD_DIM ** -0.5 * LOG2E)
    qpos = q0 + lax.broadcasted_iota(jnp.int32, (Q_TILE, SWA_KEYS), 0)
    kpos = k0 + lax.broadcasted_iota(jnp.int32, (Q_TILE, SWA_KEYS), 1)
    valid = jnp.abs(kpos - qpos) <= SWA_WINDOW
    sink = [sink_ref[2 * j + g] * LOG2E for g in range(2)]
    out = _attend_pair(q2, keys_ref, vals_ref, k0, SWA_KEYS, lambda g, s: jnp.where(valid, s, NEG_INF), sink)
    o_ref[...] = out.astype(o_ref.dtype)


def _lat_swa(p, cache, sink, layer):
    q, _, _, o = _lat_blocks((SWA_Q, SWA_K, SWA_V))
    first_k = N_CTX_TOK // DEC_SEQ
    k = pl.BlockSpec((DEC_SEQ, LANES), lambda b, j, t: (first_k + b, SWA_K // LANES))
    v = pl.BlockSpec((DEC_SEQ, LANES), lambda b, j, t: (first_k + b, SWA_V // LANES))
    cos, sin = (jnp.asarray(a) for a in _rope_tables(HEAD_DIM))
    tab = pl.BlockSpec((DEC_SEQ, LANES), lambda b, j, t: (0, 0))
    return pl.pallas_call(
        _lat_swa_kernel,
        grid=(DEC_BATCH, 2, DEC_SEQ // Q_TILE),
        in_specs=[q, k, v, _cache_spec(layer, 0, True), _cache_spec(layer, 1, True), tab, tab,
                  pl.BlockSpec(memory_space=pltpu.SMEM)],
        out_specs=o,
        out_shape=jax.ShapeDtypeStruct((N_LAT_TOK, 256), BF16),
        scratch_shapes=_staging_scratch(),
        compiler_params=_params(("arbitrary",) * 3, VMEM_LIMIT),
        name="lat_swa",
    )(p, p, p, cache, cache, cos, sin, sink)


DIFF_KEYS = DEC_SEQ + PAST_LEN
KEY_CHUNK = 256


def _lat_diff_kernel(*refs, lam_init, n_cast):
    q_ref, k_ref, v_ref, ck_ref, cv_ref, cos_ref, sin_ref, lp_ref, sg_ref = refs[:9]
    w_refs = refs[9:9 + n_cast]
    o_ref = refs[9 + n_cast]
    wb_refs = refs[10 + n_cast:10 + 2 * n_cast]
    keys_ref, vals_ref, s_ref = refs[10 + 2 * n_cast:]
    t = pl.program_id(2)
    eighth = DIFF_QK_DIM // 4

    for w_ref, wb_ref in zip(w_refs, wb_refs):
        wb_ref[...] = w_ref[...].astype(BF16)

    @pl.when(t == 0)
    def _():
        _stage_keys_values(keys_ref, vals_ref, _rope(k_ref[...], cos_ref[...], sin_ref[...], eighth), v_ref[...],
                           ck_ref[...], cv_ref[...])

    q0 = pl.multiple_of(t * Q_TILE, Q_TILE)
    q2 = _rope(q_ref[...], cos_ref[pl.ds(q0, Q_TILE), :], sin_ref[pl.ds(q0, Q_TILE), :], eighth)
    q2 = q2 * (DIFF_QK_DIM ** -0.5 * LOG2E)
    lam = _lambda(lp_ref[...], lam_init)
    lane = _lane(q2.shape)
    quarter = lane // DIFF_QK_DIM
    half = lane // HEAD_DIM
    n_chunks = DIFF_KEYS // KEY_CHUNK

    def scores(i):
        qm = jnp.where(quarter == i, q2, 0.0).astype(BF16)
        m_run = None
        for n in range(n_chunks):
            cols = slice(n * KEY_CHUNK, (n + 1) * KEY_CHUNK)
            s = _dot_nt(qm, keys_ref[cols, :])
            s_ref[i, :, cols] = s
            m_blk = jnp.maximum(s[:, :LANES], s[:, LANES:])
            m_run = m_blk if m_run is None else jnp.maximum(m_run, m_blk)
        return jnp.max(m_run, axis=-1, keepdims=True)

    def values(i, m):
        g = i // 2
        acc = None
        for n in range(n_chunks):
            cols = slice(n * KEY_CHUNK, (n + 1) * KEY_CHUNK)
            part = _dot(jnp.exp2(s_ref[i, :, cols] - m).astype(BF16), vals_ref[g, cols, :])
            acc = part if acc is None else acc + part
        return acc / jnp.where(half == g, pltpu.roll(acc, HEAD_DIM, 1), 1.0)

    o = []
    m_next = scores(0)
    for i in range(4):
        m = m_next
        if i + 1 < 4:
            m_next = scores(i + 1)
        o.append(values(i, m))
    outs = [o[0] - lam * o[1], o[2] - lam * o[3]]
    o2 = jnp.where(half == 0, outs[0], outs[1])
    o_ref[...] = _subln(o2, sg_ref[...], lam_init).astype(o_ref.dtype)


def _lat_diff(p, cache, lp, sg2, layer, lam_init, casts):
    q, k, v, o = _lat_blocks((DIFF_Q, DIFF_K, DIFF_V))
    cos, sin = (jnp.asarray(a) for a in _rope_tables(DIFF_QK_DIM))
    tab = pl.BlockSpec((DEC_SEQ, LANES), lambda b, j, t: (0, 0))
    n_q = DEC_SEQ // Q_TILE
    n_steps = DEC_BATCH * 2 * n_q
    cast_in, cast_out, cast_shape = [], [], []
    for w, w_layer, block_rows in casts:
        _, rows, cols = w.shape
        per_block = n_steps // (rows // block_rows)

        def block(b, j, t, per_block=per_block):
            return ((b * 2 + j) * n_q + t) // per_block

        cast_in.append(pl.BlockSpec((None, block_rows, cols),
                                    lambda b, j, t, w_layer=w_layer, block=block: (w_layer, block(b, j, t), 0)))
        cast_out.append(pl.BlockSpec((block_rows, cols), lambda b, j, t, block=block: (block(b, j, t), 0)))
        cast_shape.append(jax.ShapeDtypeStruct((rows, cols), BF16))
    return pl.pallas_call(
        functools.partial(_lat_diff_kernel, lam_init=lam_init, n_cast=len(casts)),
        grid=(DEC_BATCH, 2, n_q),
        in_specs=[q, k, v, _cache_spec(layer, 0, False), _cache_spec(layer, 1, False), tab, tab,
                  pl.BlockSpec((4, DIFF_QK_DIM), lambda b, j, t: (0, 0)),
                  pl.BlockSpec((1, LANES), lambda b, j, t: (0, 0))] + cast_in,
        out_specs=[o] + cast_out,
        out_shape=[jax.ShapeDtypeStruct((N_LAT_TOK, 256), BF16)] + cast_shape,
        scratch_shapes=_staging_scratch() + [pltpu.VMEM((4, Q_TILE, DIFF_KEYS), F32)],
        compiler_params=_params(("arbitrary",) * 3, VMEM_LIMIT),
        name="lat_diff",
    )(p, p, p, cache, cache, cos, sin, lp, sg2, *[w for w, _, _ in casts])


def kernel(x_prompt, x_sample, cache_na_kv, cache_swa_kv, cache_diff_kv, c, c_ctx, norm1_g, norm2_g, ada_w,
           ada_b, w_in, na_rpb, swa_sink, diff_lambda, diff_subln_g, w_branch, w_gate, b_gate, w_o, w_ffn_in,
           w_ffn_out, final_norm_g):
    x = (x_prompt.reshape(N_CTX_TOK, D_MODEL), x_sample.reshape(N_LAT_TOK, D_MODEL))
    cond = jnp.zeros((COND_ROWS, D_MODEL), F32).at[0].set(c_ctx).at[1:1 + DEC_BATCH].set(c)
    mod = _adaln(cond, ada_w, ada_b).reshape(DEPTH, COND_ROWS, 6, D_MODEL)

    cache_na = cache_na_kv.reshape(DEC_BATCH, DEPTH, 2, PAST_LEN, 256)
    cache_swa = cache_swa_kv.reshape(DEC_BATCH, DEPTH, 2, PAST_LEN, LANES)
    cache_diff = cache_diff_kv.reshape(DEC_BATCH, DEPTH, 2, PAST_LEN, 256)
    fg = final_norm_g.reshape(1, D_MODEL)

    states = None
    w_in_b = w_in[0].astype(BF16)
    w_branch2 = w_branch.reshape(DEPTH, D_MODEL, D_MODEL)
    for l in range(DEPTH):
        lam_init = 0.8 - 0.6 * math.exp(-0.3 * l)
        g1 = norm1_g[l].reshape(1, D_MODEL)
        sg2 = jnp.tile(diff_subln_g[l], 2).reshape(1, LANES)
        p, h, *states = _proj(x, mod, l, g1, w_in_b, states)

        c_na, c_swa, c_f, c_diff = _ctx_mixers(p, swa_sink[l], diff_lambda[l], sg2, lam_init)
        l_na = _lat_na(p, cache_na, _na_bias_rows(na_rpb[l]), l)
        l_swa = _lat_swa(p, cache_swa, swa_sink[l], l)
        l_f = _fourier_latent(p)
        casts = [(w_gate, l, 16), (w_branch2, l, 16), (w_o, l, 16), (w_ffn_in, l, 16), (w_ffn_out, l, 176)]
        if l + 1 < DEPTH:
            casts.append((w_in, l + 1, 16))
        l_diff, wg_b, wb_b, wo_b, wi_b, wout_b, *nxt = _lat_diff(p, cache_diff, diff_lambda[l], sg2, l, lam_init,
                                                                  casts)
        if nxt:
            w_in_b = nxt[0]

        x = _merge(x, mod, l, h, (c_na, c_swa, c_f, c_diff), (l_na, l_swa, l_f, l_diff),
                   wg_b, b_gate[l].reshape(1, -1), wb_b.reshape(N_BRANCH, BRANCH_DIM, D_MODEL), wo_b)
        x = _ffn(x, mod, l, norm2_g[l].reshape(1, D_MODEL), wi_b, wout_b, fg, final=(l == DEPTH - 1))

    y_ctx, y_lat = x
    na_kv, swa_kv, diff_kv = states
    return (y_ctx.reshape(BATCH, SEQ, D_MODEL), y_lat.reshape(DEC_BATCH, DEC_SEQ, D_MODEL),
            na_kv.reshape(BATCH, DEPTH, 2, SEQ, 4, HEAD_DIM), swa_kv.reshape(BATCH, DEPTH, 2, SEQ, 2, HEAD_DIM),
            diff_kv.reshape(BATCH, DEPTH, 2, SEQ, 4, HEAD_DIM))
```

```python
import functools
import math

import numpy as np
import jax
import jax.numpy as jnp
from jax import lax
from jax.experimental import pallas as pl
from jax.experimental.pallas import tpu as pltpu

D_MODEL = 1024
BATCH = 16
SEQ = 256
DEPTH = 2
DEC_BATCH = 4
DEC_SEQ = 2048
PAST_LEN = 512
GRID_W = 64
GRID_H = DEC_SEQ // GRID_W
HEAD_DIM = 64
NA_WIN_H = 8
NA_WIN_W = 16
SWA_WINDOW = 128
DIFF_QK_DIM = 32
D_FF = 2816
D_IN = 2304
ROPE_BASE = 10000.0
NORM_EPS = 1e-6
NEG_INF = -1e30
LOG2E = math.log2(math.e)

NA_Q, NA_K, NA_V = 0, 256, 512
SWA_Q, SWA_K, SWA_V = 768, 1024, 1152
FNET_U = 1280
DIFF_Q, DIFF_K, DIFF_V = 1536, 1792, 2048

LANES = 128
N_CTX_TOK = BATCH * SEQ
N_LAT_TOK = DEC_BATCH * DEC_SEQ
N_TOK = N_CTX_TOK + N_LAT_TOK
COND_ROWS = 8
TOKEN_TILE = 512
Q_TILE = 256
MIX_TILES = 4
NA_TILE_ROWS = Q_TILE // GRID_W
NA_BAND_ROWS = NA_WIN_H + NA_TILE_ROWS
SWA_KEYS = 2 * Q_TILE
VMEM_LIMIT = 56 * 1024 * 1024

F32 = jnp.float32
BF16 = jnp.bfloat16


def _params(semantics, vmem=None):
    return pltpu.CompilerParams(dimension_semantics=semantics, vmem_limit_bytes=vmem)


def _dot(a, b):
    return jnp.dot(a, b, preferred_element_type=F32)


def _dot_nt(a, b):
    return lax.dot_general(a, b, (((1,), (1,)), ((), ())), preferred_element_type=F32)


def _split(x):
    hi = x.astype(BF16)
    lo = (x - hi.astype(F32)).astype(BF16)
    return hi, lo


def _dot3(a_hi, a_lo, b_hi, b_lo):
    return _dot(a_hi, b_hi) + _dot(a_lo, b_hi) + _dot(a_hi, b_lo)


def _sigmoid(x):
    return 1.0 / (1.0 + jnp.exp(-x))


def _rms(x, g):
    return x * lax.rsqrt(jnp.mean(x * x, axis=-1, keepdims=True) + NORM_EPS) * g


def _exp_parts(blocks, extra=None):
    m = None
    for s in blocks:
        mi = jnp.max(s, axis=-1, keepdims=True)
        m = mi if m is None else jnp.maximum(m, mi)
    if extra is not None:
        m = jnp.maximum(m, extra)
    es = [jnp.exp2(s - m) for s in blocks]
    l = None
    for e in es:
        li = jnp.sum(e, axis=-1, keepdims=True)
        l = li if l is None else l + li
    if extra is not None:
        l = l + jnp.exp2(extra - m)
    return es, l


def _lane(shape):
    return lax.broadcasted_iota(jnp.int32, shape, 1)


def _rope(x, cos, sin_signed, half):
    lane = _lane(x.shape)
    partner = jnp.where((lane % (2 * half)) < half,
                        pltpu.roll(x, LANES - half, 1), pltpu.roll(x, half, 1))
    return x * cos + partner * sin_signed


def _lambda(lp, lam_init):
    s1 = jnp.sum(lp[0:1, :] * lp[1:2, :], axis=-1, keepdims=True)
    s2 = jnp.sum(lp[2:3, :] * lp[3:4, :], axis=-1, keepdims=True)
    return jnp.exp(s1) - jnp.exp(s2) + lam_init


def _subln(o, g2, lam_init):
    lane = _lane(o.shape)
    sq = o * o
    ms0 = jnp.sum(jnp.where(lane < HEAD_DIM, sq, 0.0), axis=-1, keepdims=True)
    ms1 = jnp.sum(jnp.where(lane >= HEAD_DIM, sq, 0.0), axis=-1, keepdims=True)
    ms = jnp.where(lane < HEAD_DIM, ms0, ms1) * (1.0 / HEAD_DIM)
    return (o * lax.rsqrt(ms + NORM_EPS) * g2) * (1.0 - lam_init)


def _adaln_kernel(cond_ref, w_ref, b_ref, o_ref):
    c = cond_ref[...]
    s = c * _sigmoid(c)
    s_hi, s_lo = _split(s)
    w_hi, w_lo = _split(w_ref[...])
    o_ref[...] = _dot3(s_hi, s_lo, w_hi, w_lo) + b_ref[...]


def _adaln(cond, ada_w, ada_b):
    tn = 1536
    n = 6 * D_MODEL
    return pl.pallas_call(
        _adaln_kernel,
        grid=(DEPTH, n // tn),
        in_specs=[
            pl.BlockSpec((COND_ROWS, D_MODEL), lambda l, j: (0, 0)),
            pl.BlockSpec((None, D_MODEL, tn), lambda l, j: (l, 0, j)),
            pl.BlockSpec((None, 1, tn), lambda l, j: (l, 0, j)),
        ],
        out_specs=pl.BlockSpec((None, COND_ROWS, tn), lambda l, j: (l, 0, j)),
        out_shape=jax.ShapeDtypeStruct((DEPTH, COND_ROWS, n), F32),
        compiler_params=_params(("arbitrary", "arbitrary")),
        name="adaln",
    )(cond, ada_w, ada_b.reshape(DEPTH, 1, n))


def _cond_row(i):
    n_ctx = N_CTX_TOK // TOKEN_TILE
    per_seq = DEC_SEQ // TOKEN_TILE
    return jnp.where(i < n_ctx, 0, 1 + (i - n_ctx) // per_seq)


def _mod_spec(layer):
    return pl.BlockSpec((None, None, 6, D_MODEL), lambda i: (layer, _cond_row(i), 0, 0))


ROW_SPLIT = 2
N_CTX_TILES = N_CTX_TOK // TOKEN_TILE
N_LAT_TILES = N_LAT_TOK // TOKEN_TILE


def _tok_spec(width):
    return pl.BlockSpec((TOKEN_TILE, width), lambda i: (i, 0))


def _ctx_tile_spec(width):
    return pl.BlockSpec((TOKEN_TILE, width), lambda i: (jnp.minimum(i, N_CTX_TILES - 1), 0))


def _lat_tile_spec(width):
    return pl.BlockSpec((TOKEN_TILE, width), lambda i: (jnp.clip(i - N_CTX_TILES, 0, N_LAT_TILES - 1), 0))


def _const_spec(shape):
    nd = len(shape)
    return pl.BlockSpec(shape, lambda i: (0,) * nd, pipeline_mode=pl.Buffered(1))


def _is_ctx_tile():
    return pl.program_id(0) < N_CTX_TILES


def _x_specs(x):
    return [_ctx_tile_spec(D_MODEL), _lat_tile_spec(D_MODEL)] if isinstance(x, tuple) else [_tok_spec(D_MODEL)]


def _x_args(x):
    return list(x) if isinstance(x, tuple) else [x]


def _load_x(x_refs, rows=slice(None)):
    if len(x_refs) == 1:
        return x_refs[0][rows, :]
    return jnp.where(_is_ctx_tile(), x_refs[0][rows, :], x_refs[1][rows, :])


def _proj_kernel(*refs, n_x, layer, first):
    x_refs = refs[:n_x]
    mod_ref, g_ref, w_ref = refs[n_x:n_x + 3]
    p_ref, h_ref, na_ref, swa_ref, diff_ref = refs[-5:]
    for r in range(ROW_SPLIT):
        rows = slice(r * TOKEN_TILE // ROW_SPLIT, (r + 1) * TOKEN_TILE // ROW_SPLIT)
        h = _rms(_load_x(x_refs, rows), g_ref[...]) * (1.0 + mod_ref[1:2, :]) + mod_ref[0:1, :]
        hb = h.astype(BF16)
        h_ref[rows, :] = hb
        p_ref[rows, :] = _dot(hb, w_ref[...])

    @pl.when(_is_ctx_tile())
    def _():
        for s in range(TOKEN_TILE // SEQ):
            rows = slice(s * SEQ, (s + 1) * SEQ)
            for ref, col, width in ((na_ref, NA_K, 256), (swa_ref, SWA_K, LANES), (diff_ref, DIFF_K, 256)):
                for which in range(2):
                    data = p_ref[rows, col + width * which: col + width * (which + 1)]
                    if first:
                        ref[s, layer, which] = data
                    else:
                        ref[s, which] = data
                if first:
                    for other in range(DEPTH):
                        if other != layer:
                            ref[s, other] = jnp.zeros((2, SEQ, width), F32)


STATE_WIDTHS = (256, LANES, 256)


def _proj(x, mod, layer, g, w, states):
    seqs = TOKEN_TILE // SEQ
    first = states is None

    def state_spec(width):
        if first:
            return pl.BlockSpec((seqs, DEPTH, 2, SEQ, width), lambda i: (jnp.minimum(i, N_CTX_TILES - 1), 0, 0, 0, 0))
        return pl.BlockSpec((seqs, None, 2, SEQ, width),
                            lambda i: (jnp.minimum(i, N_CTX_TILES - 1), layer, 0, 0, 0))

    n_x = len(_x_args(x))
    n_in = n_x + 3
    prior = [] if first else list(states)
    return pl.pallas_call(
        functools.partial(_proj_kernel, n_x=n_x, layer=layer, first=first),
        grid=(N_TOK // TOKEN_TILE,),
        in_specs=_x_specs(x) + [_mod_spec(layer), _const_spec((1, D_MODEL)), _const_spec((D_MODEL, D_IN))]
                 + [pl.BlockSpec(memory_space=pl.ANY)] * len(prior),
        out_specs=[_tok_spec(D_IN), _tok_spec(D_MODEL)] + [state_spec(w_) for w_ in STATE_WIDTHS],
        out_shape=[jax.ShapeDtypeStruct((N_TOK, D_IN), F32), jax.ShapeDtypeStruct((N_TOK, D_MODEL), BF16)]
                  + [jax.ShapeDtypeStruct((BATCH, DEPTH, 2, SEQ, w_), F32) for w_ in STATE_WIDTHS],
        input_output_aliases={n_in + k: 2 + k for k in range(len(prior))},
        compiler_params=_params(("arbitrary",), VMEM_LIMIT),
        name="proj",
    )(*_x_args(x), mod, g, w, *prior)


N_BRANCH = 4
BRANCH_DIM = D_MODEL // N_BRANCH


def _merge_kernel(*refs, n_x):
    x_refs = refs[:n_x]
    mod_ref, h_ref = refs[n_x:n_x + 2]
    ctx_refs = refs[n_x + 2:n_x + 2 + N_BRANCH]
    lat_refs = refs[n_x + 2 + N_BRANCH:n_x + 2 + 2 * N_BRANCH]
    wg_ref, bg_ref, wb_ref, wo_ref, o_ref = refs[n_x + 2 + 2 * N_BRANCH:]
    is_ctx = _is_ctx_tile()
    for r in range(ROW_SPLIT):
        rows = slice(r * TOKEN_TILE // ROW_SPLIT, (r + 1) * TOKEN_TILE // ROW_SPLIT)
        hb = h_ref[rows, :]
        merged = None
        for k in range(N_BRANCH):
            cols = slice(k * D_MODEL, (k + 1) * D_MODEL)
            gate = _sigmoid(_dot(hb, wg_ref[:, cols]) + bg_ref[:, cols])
            branch = jnp.where(is_ctx, ctx_refs[k][rows, :], lat_refs[k][rows, :])
            term = gate * _dot(branch, wb_ref[k])
            merged = term if merged is None else merged + term
        o_ref[rows, :] = _load_x(x_refs, rows) + mod_ref[2:3, :] * _dot(merged.astype(BF16), wo_ref[...])


def _merge(x, mod, layer, h, ctx_branches, lat_branches, wg, bg, wb, wo):
    return pl.pallas_call(
        functools.partial(_merge_kernel, n_x=len(_x_args(x))),
        grid=(N_TOK // TOKEN_TILE,),
        in_specs=_x_specs(x) + [_mod_spec(layer), _tok_spec(D_MODEL)]
                 + [_ctx_tile_spec(BRANCH_DIM)] * N_BRANCH + [_lat_tile_spec(BRANCH_DIM)] * N_BRANCH
                 + [_const_spec((D_MODEL, N_BRANCH * D_MODEL)), _const_spec((1, N_BRANCH * D_MODEL)),
                    _const_spec((N_BRANCH, BRANCH_DIM, D_MODEL)), _const_spec((D_MODEL, D_MODEL))],
        out_specs=_tok_spec(D_MODEL),
        out_shape=jax.ShapeDtypeStruct((N_TOK, D_MODEL), F32),
        compiler_params=_params(("arbitrary",), VMEM_LIMIT),
        name="merge",
    )(*_x_args(x), mod, h, *ctx_branches, *lat_branches, wg, bg, wb, wo)


def _ffn_kernel(x_ref, mod_ref, g_ref, wi_ref, wo_ref, fg_ref, *o_refs, final):
    ys = []
    for r in range(ROW_SPLIT):
        rows = slice(r * TOKEN_TILE // ROW_SPLIT, (r + 1) * TOKEN_TILE // ROW_SPLIT)
        x = x_ref[rows, :]
        h = _rms(x, g_ref[...]) * (1.0 + mod_ref[4:5, :]) + mod_ref[3:4, :]
        hb = h.astype(BF16)
        a = _dot(hb, wi_ref[:, :D_FF])
        b = _dot(hb, wi_ref[:, D_FF:])
        f = (a * _sigmoid(a)) * b
        y = x + mod_ref[5:6, :] * _dot(f.astype(BF16), wo_ref[...])
        if not final:
            o_refs[0][rows, :] = y
        else:
            ys.append(_rms(y, fg_ref[...]))
    if not final:
        return
    y = jnp.concatenate(ys, axis=0)
    ctx_ref, lat_ref = o_refs

    @pl.when(_is_ctx_tile())
    def _():
        ctx_ref[...] = y

    @pl.when(jnp.logical_not(_is_ctx_tile()))
    def _():
        lat_ref[...] = y


def _ffn(x, mod, layer, g, wi, wo, fg, final):
    if final:
        out_specs = [_ctx_tile_spec(D_MODEL), _lat_tile_spec(D_MODEL)]
        out_shape = [jax.ShapeDtypeStruct((N_CTX_TOK, D_MODEL), F32),
                     jax.ShapeDtypeStruct((N_LAT_TOK, D_MODEL), F32)]
    else:
        out_specs = _tok_spec(D_MODEL)
        out_shape = jax.ShapeDtypeStruct((N_TOK, D_MODEL), F32)
    return pl.pallas_call(
        functools.partial(_ffn_kernel, final=final),
        grid=(N_TOK // TOKEN_TILE,),
        in_specs=[_tok_spec(D_MODEL), _mod_spec(layer), _const_spec((1, D_MODEL)),
                  _const_spec((D_MODEL, 2 * D_FF)), _const_spec((D_FF, D_MODEL)),
                  _const_spec((1, D_MODEL))],
        out_specs=out_specs,
        out_shape=out_shape,
        compiler_params=_params(("arbitrary",), VMEM_LIMIT),
        name="ffn",
    )(x, mod, g, wi, wo, fg)


@functools.lru_cache(maxsize=None)
def _dft_tables(n, blocks):
    k = np.arange(n, dtype=np.int64)
    ang = 2.0 * np.pi * ((k[:, None] * k[None, :]) % n).astype(np.float64) / n
    out = []
    for m in (np.cos(ang), np.sin(ang)):
        m = np.kron(np.eye(blocks), m / math.sqrt(n)).astype(np.float32)
        hi = m.astype(BF16)
        lo = (m - hi.astype(np.float32)).astype(BF16)
        out += [hi, lo]
    return tuple(out)


def _fourier_direct(u, pos_refs, chan_refs):
    ch_ref, cl_ref, sh_ref, sl_ref = pos_refs
    cch_ref, ccl_ref, sch_ref, scl_ref = chan_refs
    u_hi, u_lo = _split(u)
    a_hi, a_lo = _split(_dot3(ch_ref[...], cl_ref[...], u_hi, u_lo))
    b_hi, b_lo = _split(_dot3(sh_ref[...], sl_ref[...], u_hi, u_lo))
    return _dot3(a_hi, a_lo, cch_ref[...], ccl_ref[...]) - _dot3(b_hi, b_lo, sch_ref[...], scl_ref[...])


FFT_RADIX = 8
FFT_INNER = DEC_SEQ // FFT_RADIX


@functools.lru_cache(maxsize=None)
def _twiddle_tables():
    r = np.arange(FFT_RADIX, dtype=np.float64)[:, None]
    f2 = np.arange(FFT_INNER, dtype=np.float64)[None, :]
    ang = 2.0 * np.pi * r * f2 / DEC_SEQ
    scale = 1.0 / math.sqrt(FFT_RADIX)
    shape = (FFT_RADIX, FFT_INNER, 256)
    return tuple(np.ascontiguousarray(np.broadcast_to((f(ang) * scale)[:, :, None], shape)).astype(np.float32)
                 for f in (np.cos, np.sin))


def _fourier_latent_kernel(u0_ref, u1_ref, ch_ref, cl_ref, sh_ref, sl_ref, twc_ref, tws_ref, cch_ref, ccl_ref,
                           sch_ref, scl_ref, o_ref, tre_ref, tim_ref):
    for r in range(FFT_RADIX):
        rows = pl.ds(r, FFT_INNER, stride=FFT_RADIX)
        x_hi, x_lo = _split(jnp.concatenate([u0_ref[rows, :], u1_ref[rows, :]], axis=1))
        g_re = _dot3(ch_ref[...], cl_ref[...], x_hi, x_lo)
        g_im = -_dot3(sh_ref[...], sl_ref[...], x_hi, x_lo)
        wc, ws = twc_ref[r], tws_ref[r]
        tre_ref[r] = g_re * wc + g_im * ws
        tim_ref[r] = g_im * wc - g_re * ws

    def axpy(acc, coef, x):
        if abs(coef) < 1e-9:
            return acc
        term = x if abs(coef - 1.0) < 1e-9 else (-x if abs(coef + 1.0) < 1e-9 else coef * x)
        return term if acc is None else acc + term

    for f1 in range(FFT_RADIX):
        p_re = p_im = None
        for r in range(FFT_RADIX):
            ang = 2.0 * math.pi * ((f1 * r) % FFT_RADIX) / FFT_RADIX
            a, b = math.cos(ang), math.sin(ang)
            t_re, t_im = tre_ref[r], tim_ref[r]
            p_re = axpy(axpy(p_re, a, t_re), b, t_im)
            p_im = axpy(axpy(p_im, a, t_im), -b, t_re)
        re_hi, re_lo = _split(p_re)
        im_hi, im_lo = _split(p_im)
        y = _dot3(re_hi, re_lo, cch_ref[...], ccl_ref[...]) + _dot3(im_hi, im_lo, sch_ref[...], scl_ref[...])
        o_ref[f1 * FFT_INNER:(f1 + 1) * FFT_INNER, :] = y.astype(o_ref.dtype)


def _fourier_latent(p):
    width = 256
    first_blk = N_CTX_TOK // DEC_SEQ
    inner = [jnp.asarray(t) for t in _dft_tables(FFT_INNER, 1)]
    chan = [jnp.asarray(t) for t in _dft_tables(width // 4, 4)]
    tw = [jnp.asarray(t) for t in _twiddle_tables()]

    def u_spec(half):
        return pl.BlockSpec((DEC_SEQ, LANES), lambda b: (first_blk + b, FNET_U // LANES + half))

    def table(shape):
        nd = len(shape)
        return pl.BlockSpec(shape, lambda b: (0,) * nd, pipeline_mode=pl.Buffered(1))

    return pl.pallas_call(
        _fourier_latent_kernel,
        grid=(DEC_BATCH,),
        in_specs=[u_spec(0), u_spec(1)] + [table((FFT_INNER, FFT_INNER))] * 4
                 + [table((FFT_RADIX, FFT_INNER, width))] * 2 + [table((width, width))] * 4,
        out_specs=pl.BlockSpec((DEC_SEQ, width), lambda b: (b, 0)),
        out_shape=jax.ShapeDtypeStruct((N_LAT_TOK, width), BF16),
        scratch_shapes=[pltpu.VMEM((FFT_RADIX, FFT_INNER, width), F32)] * 2,
        compiler_params=_params(("arbitrary",), VMEM_LIMIT),
        name="fourier_lat",
    )(p, p, *inner, *tw, *chan)


def _ctx_mixer_kernel(p_ref, sink_ref, lp_ref, sg_ref, *refs, lam_init):
    pos_refs, chan_refs = refs[:4], refs[4:8]
    na_ref, swa_ref, fnet_ref, diff_ref = refs[8:]
    fnet_ref[...] = _fourier_direct(p_ref[:, FNET_U:FNET_U + 256], pos_refs, chan_refs).astype(fnet_ref.dtype)

    shape = (SEQ, LANES)
    lane = _lane(shape)
    half = lane // HEAD_DIM
    quarter = lane // DIFF_QK_DIM
    scale = HEAD_DIM ** -0.5 * LOG2E

    def pair(col, j):
        return p_ref[:, col + LANES * j: col + LANES * (j + 1)]

    for j in range(2):
        q2 = pair(NA_Q, j) * scale
        kb = pair(NA_K, j).astype(BF16)
        vb = pair(NA_V, j).astype(BF16)
        outs = []
        for g in range(2):
            qm = jnp.where(half == g, q2, 0.0).astype(BF16)
            (e,), l = _exp_parts([_dot_nt(qm, kb)])
            outs.append(_dot(e.astype(BF16), vb) / l)
        na_ref[:, LANES * j: LANES * (j + 1)] = jnp.where(half == 0, outs[0], outs[1]).astype(na_ref.dtype)

    k2 = p_ref[:, SWA_K: SWA_K + LANES]
    v2 = p_ref[:, SWA_V: SWA_V + LANES]
    k2s = pltpu.roll(k2, HEAD_DIM, 1)
    v2s = pltpu.roll(v2, HEAD_DIM, 1)
    for j in range(2):
        kb = jnp.where(half == j, k2, k2s).astype(BF16)
        vb = jnp.where(half == j, v2, v2s).astype(BF16)
        q2 = pair(SWA_Q, j) * scale
        outs = []
        for g in range(2):
            qm = jnp.where(half == g, q2, 0.0).astype(BF16)
            (e,), l = _exp_parts([_dot_nt(qm, kb)], extra=sink_ref[2 * j + g] * LOG2E)
            outs.append(_dot(e.astype(BF16), vb) / l)
        swa_ref[:, LANES * j: LANES * (j + 1)] = jnp.where(half == 0, outs[0], outs[1]).astype(swa_ref.dtype)

    lam = _lambda(lp_ref[...], lam_init)
    for j in range(2):
        q2 = pair(DIFF_Q, j) * (DIFF_QK_DIM ** -0.5 * LOG2E)
        kb = pair(DIFF_K, j).astype(BF16)
        vb = pair(DIFF_V, j).astype(BF16)
        outs = []
        for g in range(2):
            o = []
            for c in range(2):
                qm = jnp.where(quarter == 2 * g + c, q2, 0.0).astype(BF16)
                (e,), l = _exp_parts([_dot_nt(qm, kb)])
                o.append(_dot(e.astype(BF16), vb) / l)
            outs.append(o[0] - lam * o[1])
        o2 = jnp.where(half == 0, outs[0], outs[1])
        diff_ref[:, LANES * j: LANES * (j + 1)] = _subln(o2, sg_ref[...], lam_init).astype(diff_ref.dtype)


def _ctx_mixers(p, sink, lp, sg2, lam_init):
    out = jax.ShapeDtypeStruct((N_CTX_TOK, 256), BF16)
    ospec = pl.BlockSpec((SEQ, 256), lambda b: (b, 0))
    tables = [jnp.asarray(t) for t in _dft_tables(SEQ, 1) + _dft_tables(256 // 4, 4)]
    table_spec = pl.BlockSpec((256, 256), lambda b: (0, 0), pipeline_mode=pl.Buffered(1))
    return pl.pallas_call(
        functools.partial(_ctx_mixer_kernel, lam_init=lam_init),
        grid=(BATCH,),
        in_specs=[pl.BlockSpec((SEQ, D_IN), lambda b: (b, 0)),
                  pl.BlockSpec(memory_space=pltpu.SMEM),
                  pl.BlockSpec((4, DIFF_QK_DIM), lambda b: (0, 0)),
                  pl.BlockSpec((1, LANES), lambda b: (0, 0))] + [table_spec] * 8,
        out_specs=[ospec] * 4,
        out_shape=[out] * 4,
        compiler_params=_params(("arbitrary",), VMEM_LIMIT),
        name="ctx_mixers",
    )(p, sink, lp, sg2, *tables)


N_DR = 2 * NA_WIN_H - 1
N_DC = 2 * NA_WIN_W - 1
N_PAIR_BLOCKS = N_DR + 1


def _na_band_start(t):
    return jnp.clip(NA_TILE_ROWS * t - NA_WIN_H // 2, 0, GRID_H - NA_BAND_ROWS)


def _na_bias_rows(rpb):
    rows = jnp.pad(rpb, ((0, 0), (1, 1), (0, GRID_W - N_DC)))
    return jnp.concatenate([rows[:, :N_PAIR_BLOCKS], rows[:, 1:]], axis=-1)


def _lat_na_kernel(q_ref, k_ref, v_ref, ck_ref, cv_ref, rows_ref, o_ref, pair_ref, keys_ref, vals_ref):
    step = pl.program_id(2)
    blk = (GRID_W, LANES)

    @pl.when(step == 0)
    def _():
        c = lax.broadcasted_iota(jnp.int32, blk, 0)
        kc = _lane(blk) % GRID_W
        cs = jnp.clip(c - NA_WIN_W // 2, 0, GRID_W - NA_WIN_W)
        in_cols = (kc >= cs) & (kc < cs + NA_WIN_W)
        for g in range(2):
            for i in range(N_PAIR_BLOCKS):
                row = jnp.broadcast_to(rows_ref[g, i:i + 1, :], blk) * LOG2E
                toeplitz = pltpu.roll(row, LANES - (NA_WIN_W - 1), 1, stride=1, stride_axis=0)
                pair_ref[g, i] = jnp.where(in_cols, toeplitz, NEG_INF)

        _stage_keys_values(keys_ref, vals_ref, k_ref[...], v_ref[...], ck_ref[...], cv_ref[...])

    def bias(t, band, g):
        rows = []
        for ri in range(NA_TILE_ROWS):
            r = NA_TILE_ROWS * t + ri
            first = jnp.clip(r - NA_WIN_H // 2, 0, GRID_H - NA_WIN_H) - band
            d0 = band - r + NA_WIN_H - 1
            blocks = []
            for m in range(NA_BAND_ROWS // 2):
                idx = jnp.clip(d0 + 2 * m + 1, 0, N_PAIR_BLOCKS - 1)
                jrow = 2 * m + _lane(blk) // GRID_W
                in_rows = (jrow >= first) & (jrow < first + NA_WIN_H)
                blocks.append(jnp.where(in_rows, pair_ref[g, idx], NEG_INF))
            rows.append(jnp.concatenate(blocks, axis=1))
        return jnp.concatenate(rows, axis=0)

    tiles = []
    for u in range(MIX_TILES):
        t = MIX_TILES * step + u
        band = _na_band_start(t)
        q2 = q_ref[u * Q_TILE:(u + 1) * Q_TILE, :] * (HEAD_DIM ** -0.5 * LOG2E)
        tiles.append((q2, pl.multiple_of(band * GRID_W, GRID_W),
                      lambda g, s, t=t, band=band: s + bias(t, band, g)))
    outs = _attend_tiles(tiles, keys_ref, vals_ref, NA_BAND_ROWS * GRID_W)
    for u, out in enumerate(outs):
        o_ref[u * Q_TILE:(u + 1) * Q_TILE, :] = out.astype(o_ref.dtype)


def _stage_keys_values(keys_ref, vals_ref, k2, v2, ck2, cv2):
    keys_ref[:DEC_SEQ, :] = k2.astype(BF16)
    keys_ref[DEC_SEQ:, :] = ck2.astype(BF16)
    for g in range(2):
        vals_ref[g, :DEC_SEQ, :] = jnp.where(_lane(v2.shape) // HEAD_DIM == g, v2, 1.0).astype(BF16)
        vals_ref[g, DEC_SEQ:, :] = jnp.where(_lane(cv2.shape) // HEAD_DIM == g, cv2, 1.0).astype(BF16)


def _attend_tiles(tiles, keys_ref, vals_ref, n_loc, sink=None):
    ctx = slice(DEC_SEQ, DEC_SEQ + PAST_LEN)
    jobs = []
    for q2, start, fix_local in tiles:
        half = _lane(q2.shape) // HEAD_DIM
        loc = pl.ds(start, n_loc)
        for g in range(2):
            qm = jnp.where(half == g, q2, 0.0).astype(BF16)
            jobs.append((g, half, loc, fix_local(g, _dot_nt(qm, keys_ref[loc, :])), _dot_nt(qm, keys_ref[ctx, :])))
    outs = []
    for g, half, loc, s_loc, s_ctx in jobs:
        m = jnp.maximum(jnp.max(s_loc, axis=-1, keepdims=True), jnp.max(s_ctx, axis=-1, keepdims=True))
        if sink is not None:
            m = jnp.maximum(m, sink[g])
        acc = (_dot(jnp.exp2(s_loc - m).astype(BF16), vals_ref[g, loc, :])
               + _dot(jnp.exp2(s_ctx - m).astype(BF16), vals_ref[g, ctx, :]))
        if sink is not None:
            acc = acc + jnp.where(half == g, 0.0, jnp.exp2(sink[g] - m))
        outs.append(acc / jnp.where(half == g, pltpu.roll(acc, HEAD_DIM, 1), 1.0))
    half = _lane(outs[0].shape) // HEAD_DIM
    return [jnp.where(half == 0, outs[2 * u], outs[2 * u + 1]) for u in range(len(tiles))]


def _staging_scratch():
    n = DEC_SEQ + PAST_LEN
    return [pltpu.VMEM((n, LANES), BF16), pltpu.VMEM((2, n, LANES), BF16)]


def _lat_blocks(col, tiles=1):
    rows = tiles * Q_TILE
    qt = DEC_SEQ // rows
    first_q = N_CTX_TOK // rows
    first_k = N_CTX_TOK // DEC_SEQ
    q = pl.BlockSpec((rows, LANES), lambda b, j, t: (first_q + qt * b + t, col[0] // LANES + j))
    k = pl.BlockSpec((DEC_SEQ, LANES), lambda b, j, t: (first_k + b, col[1] // LANES + j))
    v = pl.BlockSpec((DEC_SEQ, LANES), lambda b, j, t: (first_k + b, col[2] // LANES + j))
    o = pl.BlockSpec((rows, LANES), lambda b, j, t: (qt * b + t, j))
    return q, k, v, o


def _cache_spec(layer, which, shared_kv):
    return pl.BlockSpec((None, None, None, PAST_LEN, LANES),
                        lambda b, j, t: (b, layer, which, 0, 0 if shared_kv else j))


def _lat_na(p, cache, bias_rows, layer):
    q, k, v, o = _lat_blocks((NA_Q, NA_K, NA_V), MIX_TILES)
    rows_spec = pl.BlockSpec((2, N_PAIR_BLOCKS, LANES), lambda b, j, t: (j, 0, 0))
    return pl.pallas_call(
        _lat_na_kernel,
        grid=(DEC_BATCH, 2, DEC_SEQ // (MIX_TILES * Q_TILE)),
        in_specs=[q, k, v, _cache_spec(layer, 0, False), _cache_spec(layer, 1, False), rows_spec],
        out_specs=o,
        out_shape=jax.ShapeDtypeStruct((N_LAT_TOK, 256), BF16),
        scratch_shapes=[pltpu.VMEM((2, N_PAIR_BLOCKS, GRID_W, LANES), F32)] + _staging_scratch(),
        compiler_params=_params(("arbitrary",) * 3, VMEM_LIMIT),
        name="lat_na",
    )(p, p, p, cache, cache, bias_rows)


@functools.lru_cache(maxsize=None)
def _rope_tables(dim):
    quarter = dim // 4
    pos = np.arange(DEC_SEQ)
    rows, cols = pos // GRID_W, pos % GRID_W
    lane = np.arange(LANES)
    w = lane % dim
    axis_pos = np.where((w // (dim // 2) == 0)[None, :], rows[:, None], cols[:, None]).astype(np.float64)
    u = w % (dim // 2)
    inv = ROPE_BASE ** (-(u % quarter).astype(np.float64) * 2.0 / (dim // 2))
    ang = axis_pos * inv[None, :]
    sign = np.where(u < quarter, -1.0, 1.0)[None, :]
    return np.cos(ang).astype(np.float32), (np.sin(ang) * sign).astype(np.float32)


def _swa_key_start(t):
    return jnp.clip(Q_TILE * t - SWA_WINDOW, 0, DEC_SEQ - SWA_KEYS)


def _lat_swa_kernel(q_ref, k_ref, v_ref, ck_ref, cv_ref, cos_ref, sin_ref, sink_ref, o_ref, keys_ref, vals_ref):
    j = pl.program_id(1)
    step = pl.program_id(2)
    quarter = HEAD_DIM // 4

    @pl.when(step == 0)
    def _():
        def head_j(x):
            return jnp.where(_lane(x.shape) // HEAD_DIM == j, x, pltpu.roll(x, HEAD_DIM, 1))

        _stage_keys_values(keys_ref, vals_ref, head_j(_rope(k_ref[...], cos_ref[...], sin_ref[...], quarter)),
                           head_j(v_ref[...]), head_j(ck_ref[...]), head_j(cv_ref[...]))

    tiles = []
    for u in range(MIX_TILES):
        t = MIX_TILES * step + u
        q0 = pl.multiple_of(t * Q_TILE, Q_TILE)
        k0 = pl.multiple_of(_swa_key_start(t), SWA_WINDOW)
        q2 = _rope(q_ref[u * Q_TILE:(u + 1) * Q_TILE, :], cos_ref[pl.ds(q0, Q_TILE), :],
                   sin_ref[pl.ds(q0, Q_TILE), :], quarter)
        qpos = q0 + lax.broadcasted_iota(jnp.int32, (Q_TILE, SWA_KEYS), 0)
        kpos = k0 + lax.broadcasted_iota(jnp.int32, (Q_TILE, SWA_KEYS), 1)
        valid = jnp.abs(kpos - qpos) <= SWA_WINDOW
        tiles.append((q2 * (HEAD_DIM ** -0.5 * LOG2E), k0,
                      lambda g, s, valid=valid: jnp.where(valid, s, NEG_INF)))
    sink = [sink_ref[2 * j + g] * LOG2E for g in range(2)]
    outs = _attend_tiles(tiles, keys_ref, vals_ref, SWA_KEYS, sink)
    for u, out in enumerate(outs):
        o_ref[u * Q_TILE:(u + 1) * Q_TILE, :] = out.astype(o_ref.dtype)


def _lat_swa(p, cache, sink, layer):
    q, _, _, o = _lat_blocks((SWA_Q, SWA_K, SWA_V), MIX_TILES)
    first_k = N_CTX_TOK // DEC_SEQ
    k = pl.BlockSpec((DEC_SEQ, LANES), lambda b, j, t: (first_k + b, SWA_K // LANES))
    v = pl.BlockSpec((DEC_SEQ, LANES), lambda b, j, t: (first_k + b, SWA_V // LANES))
    cos, sin = (jnp.asarray(a) for a in _rope_tables(HEAD_DIM))
    tab = pl.BlockSpec((DEC_SEQ, LANES), lambda b, j, t: (0, 0))
    return pl.pallas_call(
        _lat_swa_kernel,
        grid=(DEC_BATCH, 2, DEC_SEQ // (MIX_TILES * Q_TILE)),
        in_specs=[q, k, v, _cache_spec(layer, 0, True), _cache_spec(layer, 1, True), tab, tab,
                  pl.BlockSpec(memory_space=pltpu.SMEM)],
        out_specs=o,
        out_shape=jax.ShapeDtypeStruct((N_LAT_TOK, 256), BF16),
        scratch_shapes=_staging_scratch(),
        compiler_params=_params(("arbitrary",) * 3, VMEM_LIMIT),
        name="lat_swa",
    )(p, p, p, cache, cache, cos, sin, sink)


DIFF_KEYS = DEC_SEQ + PAST_LEN
DIFF_TILES = 2
KEY_CHUNK = 256


def _lat_diff_kernel(*refs, lam_init, n_cast):
    q_ref, k_ref, v_ref, ck_ref, cv_ref, cos_ref, sin_ref, lp_ref, sg_ref = refs[:9]
    w_refs = refs[9:9 + n_cast]
    o_ref = refs[9 + n_cast]
    wb_refs = refs[10 + n_cast:10 + 2 * n_cast]
    keys_ref, vals_ref, s_ref = refs[10 + 2 * n_cast:]
    step = pl.program_id(2)
    eighth = DIFF_QK_DIM // 4

    for w_ref, wb_ref in zip(w_refs, wb_refs):
        wb_ref[...] = w_ref[...].astype(BF16)

    @pl.when(step == 0)
    def _():
        _stage_keys_values(keys_ref, vals_ref, _rope(k_ref[...], cos_ref[...], sin_ref[...], eighth), v_ref[...],
                           ck_ref[...], cv_ref[...])

    lam = _lambda(lp_ref[...], lam_init)
    lane = _lane((Q_TILE, LANES))
    quarter = lane // DIFF_QK_DIM
    half = lane // HEAD_DIM
    n_chunks = DIFF_KEYS // KEY_CHUNK

    q2s = []
    for u in range(DIFF_TILES):
        q0 = pl.multiple_of((DIFF_TILES * step + u) * Q_TILE, Q_TILE)
        q2 = _rope(q_ref[u * Q_TILE:(u + 1) * Q_TILE, :], cos_ref[pl.ds(q0, Q_TILE), :],
                   sin_ref[pl.ds(q0, Q_TILE), :], eighth)
        q2s.append(q2 * (DIFF_QK_DIM ** -0.5 * LOG2E))

    def scores(n_map):
        u, i = divmod(n_map, 4)
        qm = jnp.where(quarter == i, q2s[u], 0.0).astype(BF16)
        m_run = None
        for n in range(n_chunks):
            cols = slice(n * KEY_CHUNK, (n + 1) * KEY_CHUNK)
            s = _dot_nt(qm, keys_ref[cols, :])
            s_ref[n_map, :, cols] = s
            m_blk = jnp.maximum(s[:, :LANES], s[:, LANES:])
            m_run = m_blk if m_run is None else jnp.maximum(m_run, m_blk)
        return jnp.max(m_run, axis=-1, keepdims=True)

    def values(n_map, m):
        g = (n_map % 4) // 2
        acc = None
        for n in range(n_chunks):
            cols = slice(n * KEY_CHUNK, (n + 1) * KEY_CHUNK)
            part = _dot(jnp.exp2(s_ref[n_map, :, cols] - m).astype(BF16), vals_ref[g, cols, :])
            acc = part if acc is None else acc + part
        return acc / jnp.where(half == g, pltpu.roll(acc, HEAD_DIM, 1), 1.0)

    n_maps = 4 * DIFF_TILES
    o = []
    m_next = scores(0)
    for n_map in range(n_maps):
        m = m_next
        if n_map + 1 < n_maps:
            m_next = scores(n_map + 1)
        o.append(values(n_map, m))
    for u in range(DIFF_TILES):
        o0, o1, o2, o3 = o[4 * u:4 * u + 4]
        pair = jnp.where(half == 0, o0 - lam * o1, o2 - lam * o3)
        o_ref[u * Q_TILE:(u + 1) * Q_TILE, :] = _subln(pair, sg_ref[...], lam_init).astype(o_ref.dtype)


def _lat_diff(p, cache, lp, sg2, layer, lam_init, casts):
    q, k, v, o = _lat_blocks((DIFF_Q, DIFF_K, DIFF_V), DIFF_TILES)
    cos, sin = (jnp.asarray(a) for a in _rope_tables(DIFF_QK_DIM))
    tab = pl.BlockSpec((DEC_SEQ, LANES), lambda b, j, t: (0, 0))
    n_q = DEC_SEQ // (DIFF_TILES * Q_TILE)
    n_steps = DEC_BATCH * 2 * n_q
    cast_in, cast_out, cast_shape = [], [], []
    for w, w_layer, block_rows in casts:
        _, rows, cols = w.shape
        per_block = n_steps // (rows // block_rows)

        def block(b, j, t, per_block=per_block):
            return ((b * 2 + j) * n_q + t) // per_block

        cast_in.append(pl.BlockSpec((None, block_rows, cols),
                                    lambda b, j, t, w_layer=w_layer, block=block: (w_layer, block(b, j, t), 0)))
        cast_out.append(pl.BlockSpec((block_rows, cols), lambda b, j, t, block=block: (block(b, j, t), 0)))
        cast_shape.append(jax.ShapeDtypeStruct((rows, cols), BF16))
    return pl.pallas_call(
        functools.partial(_lat_diff_kernel, lam_init=lam_init, n_cast=len(casts)),
        grid=(DEC_BATCH, 2, n_q),
        in_specs=[q, k, v, _cache_spec(layer, 0, False), _cache_spec(layer, 1, False), tab, tab,
                  pl.BlockSpec((4, DIFF_QK_DIM), lambda b, j, t: (0, 0)),
                  pl.BlockSpec((1, LANES), lambda b, j, t: (0, 0))] + cast_in,
        out_specs=[o] + cast_out,
        out_shape=[jax.ShapeDtypeStruct((N_LAT_TOK, 256), BF16)] + cast_shape,
        scratch_shapes=_staging_scratch() + [pltpu.VMEM((4 * DIFF_TILES, Q_TILE, DIFF_KEYS), F32)],
        compiler_params=_params(("arbitrary",) * 3, VMEM_LIMIT),
        name="lat_diff",
    )(p, p, p, cache, cache, cos, sin, lp, sg2, *[w for w, _, _ in casts])


def kernel(x_prompt, x_sample, cache_na_kv, cache_swa_kv, cache_diff_kv, c, c_ctx, norm1_g, norm2_g, ada_w,
           ada_b, w_in, na_rpb, swa_sink, diff_lambda, diff_subln_g, w_branch, w_gate, b_gate, w_o, w_ffn_in,
           w_ffn_out, final_norm_g):
    x = (x_prompt.reshape(N_CTX_TOK, D_MODEL), x_sample.reshape(N_LAT_TOK, D_MODEL))
    cond = jnp.zeros((COND_ROWS, D_MODEL), F32).at[0].set(c_ctx).at[1:1 + DEC_BATCH].set(c)
    mod = _adaln(cond, ada_w, ada_b).reshape(DEPTH, COND_ROWS, 6, D_MODEL)

    cache_na = cache_na_kv.reshape(DEC_BATCH, DEPTH, 2, PAST_LEN, 256)
    cache_swa = cache_swa_kv.reshape(DEC_BATCH, DEPTH, 2, PAST_LEN, LANES)
    cache_diff = cache_diff_kv.reshape(DEC_BATCH, DEPTH, 2, PAST_LEN, 256)
    fg = final_norm_g.reshape(1, D_MODEL)

    states = None
    w_in_b = w_in[0].astype(BF16)
    w_branch2 = w_branch.reshape(DEPTH, D_MODEL, D_MODEL)
    for l in range(DEPTH):
        lam_init = 0.8 - 0.6 * math.exp(-0.3 * l)
        g1 = norm1_g[l].reshape(1, D_MODEL)
        sg2 = jnp.tile(diff_subln_g[l], 2).reshape(1, LANES)
        p, h, *states = _proj(x, mod, l, g1, w_in_b, states)

        c_na, c_swa, c_f, c_diff = _ctx_mixers(p, swa_sink[l], diff_lambda[l], sg2, lam_init)
        l_na = _lat_na(p, cache_na, _na_bias_rows(na_rpb[l]), l)
        l_swa = _lat_swa(p, cache_swa, swa_sink[l], l)
        l_f = _fourier_latent(p)
        casts = [(w_gate, l, 32), (w_branch2, l, 32), (w_o, l, 32), (w_ffn_in, l, 32), (w_ffn_out, l, 176)]
        if l + 1 < DEPTH:
            casts.append((w_in, l + 1, 32))
        l_diff, wg_b, wb_b, wo_b, wi_b, wout_b, *nxt = _lat_diff(p, cache_diff, diff_lambda[l], sg2, l, lam_init,
                                                                  casts)
        if nxt:
            w_in_b = nxt[0]

        x = _merge(x, mod, l, h, (c_na, c_swa, c_f, c_diff), (l_na, l_swa, l_f, l_diff),
                   wg_b, b_gate[l].reshape(1, -1), wb_b.reshape(N_BRANCH, BRANCH_DIM, D_MODEL), wo_b)
        x = _ffn(x, mod, l, norm2_g[l].reshape(1, D_MODEL), wi_b, wout_b, fg, final=(l == DEPTH - 1))

    y_ctx, y_lat = x
    na_kv, swa_kv, diff_kv = states
    return (y_ctx.reshape(BATCH, SEQ, D_MODEL), y_lat.reshape(DEC_BATCH, DEC_SEQ, D_MODEL),
            na_kv.reshape(BATCH, DEPTH, 2, SEQ, 4, HEAD_DIM), swa_kv.reshape(BATCH, DEPTH, 2, SEQ, 2, HEAD_DIM),
            diff_kv.reshape(BATCH, DEPTH, 2, SEQ, 4, HEAD_DIM))
```

```python
import functools
import math

import numpy as np
import jax
import jax.numpy as jnp
from jax import lax
from jax.experimental import pallas as pl
from jax.experimental.pallas import tpu as pltpu

D_MODEL = 1024
BATCH = 16
SEQ = 256
DEPTH = 2
DEC_BATCH = 4
DEC_SEQ = 2048
PAST_LEN = 512
GRID_W = 64
GRID_H = DEC_SEQ // GRID_W
HEAD_DIM = 64
NA_WIN_H = 8
NA_WIN_W = 16
SWA_WINDOW = 128
DIFF_QK_DIM = 32
D_FF = 2816
D_IN = 2304
ROPE_BASE = 10000.0
NORM_EPS = 1e-6
NEG_INF = -1e30
LOG2E = math.log2(math.e)

NA_Q, NA_K, NA_V = 0, 256, 512
SWA_Q, SWA_K, SWA_V = 768, 1024, 1152
FNET_U = 1280
DIFF_Q, DIFF_K, DIFF_V = 1536, 1792, 2048

LANES = 128
N_CTX_TOK = BATCH * SEQ
N_LAT_TOK = DEC_BATCH * DEC_SEQ
N_TOK = N_CTX_TOK + N_LAT_TOK
COND_ROWS = 8
TOKEN_TILE = 512
Q_TILE = 256
MIX_TILES = 8
CTX_SEQS = 2
NA_TILE_ROWS = Q_TILE // GRID_W
NA_BAND_ROWS = NA_WIN_H + NA_TILE_ROWS
SWA_KEYS = 2 * Q_TILE
VMEM_LIMIT = 56 * 1024 * 1024

F32 = jnp.float32
BF16 = jnp.bfloat16


def _params(semantics, vmem=None):
    return pltpu.CompilerParams(dimension_semantics=semantics, vmem_limit_bytes=vmem)


def _dot(a, b):
    return jnp.dot(a, b, preferred_element_type=F32)


def _dot_nt(a, b):
    return lax.dot_general(a, b, (((1,), (1,)), ((), ())), preferred_element_type=F32)


def _split(x):
    hi = x.astype(BF16)
    lo = (x - hi.astype(F32)).astype(BF16)
    return hi, lo


def _dot3(a_hi, a_lo, b_hi, b_lo):
    return _dot(a_hi, b_hi) + _dot(a_lo, b_hi) + _dot(a_hi, b_lo)


def _sigmoid(x):
    return 1.0 / (1.0 + jnp.exp(-x))


def _rms(x, g):
    return x * lax.rsqrt(jnp.mean(x * x, axis=-1, keepdims=True) + NORM_EPS) * g


def _exp_parts(blocks, extra=None):
    m = None
    for s in blocks:
        mi = jnp.max(s, axis=-1, keepdims=True)
        m = mi if m is None else jnp.maximum(m, mi)
    if extra is not None:
        m = jnp.maximum(m, extra)
    es = [jnp.exp2(s - m) for s in blocks]
    l = None
    for e in es:
        li = jnp.sum(e, axis=-1, keepdims=True)
        l = li if l is None else l + li
    if extra is not None:
        l = l + jnp.exp2(extra - m)
    return es, l


def _lane(shape):
    return lax.broadcasted_iota(jnp.int32, shape, 1)


def _rope(x, cos, sin_signed, half):
    lane = _lane(x.shape)
    partner = jnp.where((lane % (2 * half)) < half,
                        pltpu.roll(x, LANES - half, 1), pltpu.roll(x, half, 1))
    return x * cos + partner * sin_signed


def _lambda(lp, lam_init):
    s1 = jnp.sum(lp[0:1, :] * lp[1:2, :], axis=-1, keepdims=True)
    s2 = jnp.sum(lp[2:3, :] * lp[3:4, :], axis=-1, keepdims=True)
    return jnp.exp(s1) - jnp.exp(s2) + lam_init


def _subln(o, g2, lam_init):
    lane = _lane(o.shape)
    sq = o * o
    ms0 = jnp.sum(jnp.where(lane < HEAD_DIM, sq, 0.0), axis=-1, keepdims=True)
    ms1 = jnp.sum(jnp.where(lane >= HEAD_DIM, sq, 0.0), axis=-1, keepdims=True)
    ms = jnp.where(lane < HEAD_DIM, ms0, ms1) * (1.0 / HEAD_DIM)
    return (o * lax.rsqrt(ms + NORM_EPS) * g2) * (1.0 - lam_init)


def _adaln_kernel(cond_ref, w_ref, b_ref, o_ref):
    c = cond_ref[...]
    s = c * _sigmoid(c)
    s_hi, s_lo = _split(s)
    w_hi, w_lo = _split(w_ref[...])
    o_ref[...] = _dot3(s_hi, s_lo, w_hi, w_lo) + b_ref[...]


def _adaln(cond, ada_w, ada_b):
    tn = 1536
    n = 6 * D_MODEL
    return pl.pallas_call(
        _adaln_kernel,
        grid=(DEPTH, n // tn),
        in_specs=[
            pl.BlockSpec((COND_ROWS, D_MODEL), lambda l, j: (0, 0)),
            pl.BlockSpec((None, D_MODEL, tn), lambda l, j: (l, 0, j)),
            pl.BlockSpec((None, 1, tn), lambda l, j: (l, 0, j)),
        ],
        out_specs=pl.BlockSpec((None, COND_ROWS, tn), lambda l, j: (l, 0, j)),
        out_shape=jax.ShapeDtypeStruct((DEPTH, COND_ROWS, n), F32),
        compiler_params=_params(("arbitrary", "arbitrary")),
        name="adaln",
    )(cond, ada_w, ada_b.reshape(DEPTH, 1, n))


def _cond_row(i):
    n_ctx = N_CTX_TOK // TOKEN_TILE
    per_seq = DEC_SEQ // TOKEN_TILE
    return jnp.where(i < n_ctx, 0, 1 + (i - n_ctx) // per_seq)


def _mod_spec(layer):
    return pl.BlockSpec((None, None, 6, D_MODEL), lambda i: (layer, _cond_row(i), 0, 0))


ROW_SPLIT = 2
N_CTX_TILES = N_CTX_TOK // TOKEN_TILE
N_LAT_TILES = N_LAT_TOK // TOKEN_TILE


def _tok_spec(width):
    return pl.BlockSpec((TOKEN_TILE, width), lambda i: (i, 0))


def _ctx_tile_spec(width):
    return pl.BlockSpec((TOKEN_TILE, width), lambda i: (jnp.minimum(i, N_CTX_TILES - 1), 0))


def _lat_tile_spec(width):
    return pl.BlockSpec((TOKEN_TILE, width), lambda i: (jnp.clip(i - N_CTX_TILES, 0, N_LAT_TILES - 1), 0))


def _const_spec(shape):
    nd = len(shape)
    return pl.BlockSpec(shape, lambda i: (0,) * nd, pipeline_mode=pl.Buffered(1))


def _is_ctx_tile():
    return pl.program_id(0) < N_CTX_TILES


def _x_specs(x):
    return [_ctx_tile_spec(D_MODEL), _lat_tile_spec(D_MODEL)] if isinstance(x, tuple) else [_tok_spec(D_MODEL)]


def _x_args(x):
    return list(x) if isinstance(x, tuple) else [x]


def _load_x(x_refs, rows=slice(None)):
    if len(x_refs) == 1:
        return x_refs[0][rows, :]
    return jnp.where(_is_ctx_tile(), x_refs[0][rows, :], x_refs[1][rows, :])


def _proj_kernel(*refs, n_x, layer, first):
    x_refs = refs[:n_x]
    mod_ref, g_ref, w_ref = refs[n_x:n_x + 3]
    p_ref, h_ref, na_ref, swa_ref, diff_ref = refs[-5:]
    for r in range(ROW_SPLIT):
        rows = slice(r * TOKEN_TILE // ROW_SPLIT, (r + 1) * TOKEN_TILE // ROW_SPLIT)
        h = _rms(_load_x(x_refs, rows), g_ref[...]) * (1.0 + mod_ref[1:2, :]) + mod_ref[0:1, :]
        hb = h.astype(BF16)
        h_ref[rows, :] = hb
        p_ref[rows, :] = _dot(hb, w_ref[...])

    @pl.when(_is_ctx_tile())
    def _():
        for s in range(TOKEN_TILE // SEQ):
            rows = slice(s * SEQ, (s + 1) * SEQ)
            for ref, col, width in ((na_ref, NA_K, 256), (swa_ref, SWA_K, LANES), (diff_ref, DIFF_K, 256)):
                for which in range(2):
                    data = p_ref[rows, col + width * which: col + width * (which + 1)]
                    if first:
                        ref[s, layer, which] = data
                    else:
                        ref[s, which] = data
                if first:
                    for other in range(DEPTH):
                        if other != layer:
                            ref[s, other] = jnp.zeros((2, SEQ, width), F32)


STATE_WIDTHS = (256, LANES, 256)


def _proj(x, mod, layer, g, w, states):
    seqs = TOKEN_TILE // SEQ
    first = states is None

    def state_spec(width):
        if first:
            return pl.BlockSpec((seqs, DEPTH, 2, SEQ, width), lambda i: (jnp.minimum(i, N_CTX_TILES - 1), 0, 0, 0, 0))
        return pl.BlockSpec((seqs, None, 2, SEQ, width),
                            lambda i: (jnp.minimum(i, N_CTX_TILES - 1), layer, 0, 0, 0))

    n_x = len(_x_args(x))
    n_in = n_x + 3
    prior = [] if first else list(states)
    return pl.pallas_call(
        functools.partial(_proj_kernel, n_x=n_x, layer=layer, first=first),
        grid=(N_TOK // TOKEN_TILE,),
        in_specs=_x_specs(x) + [_mod_spec(layer), _const_spec((1, D_MODEL)), _const_spec((D_MODEL, D_IN))]
                 + [pl.BlockSpec(memory_space=pl.ANY)] * len(prior),
        out_specs=[_tok_spec(D_IN), _tok_spec(D_MODEL)] + [state_spec(w_) for w_ in STATE_WIDTHS],
        out_shape=[jax.ShapeDtypeStruct((N_TOK, D_IN), F32), jax.ShapeDtypeStruct((N_TOK, D_MODEL), BF16)]
                  + [jax.ShapeDtypeStruct((BATCH, DEPTH, 2, SEQ, w_), F32) for w_ in STATE_WIDTHS],
        input_output_aliases={n_in + k: 2 + k for k in range(len(prior))},
        compiler_params=_params(("arbitrary",), VMEM_LIMIT),
        name="proj",
    )(*_x_args(x), mod, g, w, *prior)


N_BRANCH = 4
BRANCH_DIM = D_MODEL // N_BRANCH


def _merge_kernel(*refs, n_x):
    x_refs = refs[:n_x]
    mod_ref, h_ref = refs[n_x:n_x + 2]
    ctx_refs = refs[n_x + 2:n_x + 2 + N_BRANCH]
    lat_refs = refs[n_x + 2 + N_BRANCH:n_x + 2 + 2 * N_BRANCH]
    wg_ref, bg_ref, wb_ref, wo_ref, o_ref = refs[n_x + 2 + 2 * N_BRANCH:]
    is_ctx = _is_ctx_tile()
    for r in range(ROW_SPLIT):
        rows = slice(r * TOKEN_TILE // ROW_SPLIT, (r + 1) * TOKEN_TILE // ROW_SPLIT)
        hb = h_ref[rows, :]
        merged = None
        for k in range(N_BRANCH):
            cols = slice(k * D_MODEL, (k + 1) * D_MODEL)
            gate = _sigmoid(_dot(hb, wg_ref[:, cols]) + bg_ref[:, cols])
            branch = jnp.where(is_ctx, ctx_refs[k][rows, :], lat_refs[k][rows, :])
            term = gate * _dot(branch, wb_ref[k])
            merged = term if merged is None else merged + term
        o_ref[rows, :] = _load_x(x_refs, rows) + mod_ref[2:3, :] * _dot(merged.astype(BF16), wo_ref[...])


def _merge(x, mod, layer, h, ctx_branches, lat_branches, wg, bg, wb, wo):
    return pl.pallas_call(
        functools.partial(_merge_kernel, n_x=len(_x_args(x))),
        grid=(N_TOK // TOKEN_TILE,),
        in_specs=_x_specs(x) + [_mod_spec(layer), _tok_spec(D_MODEL)]
                 + [_ctx_tile_spec(BRANCH_DIM)] * N_BRANCH + [_lat_tile_spec(BRANCH_DIM)] * N_BRANCH
                 + [_const_spec((D_MODEL, N_BRANCH * D_MODEL)), _const_spec((1, N_BRANCH * D_MODEL)),
                    _const_spec((N_BRANCH, BRANCH_DIM, D_MODEL)), _const_spec((D_MODEL, D_MODEL))],
        out_specs=_tok_spec(D_MODEL),
        out_shape=jax.ShapeDtypeStruct((N_TOK, D_MODEL), F32),
        compiler_params=_params(("arbitrary",), VMEM_LIMIT),
        name="merge",
    )(*_x_args(x), mod, h, *ctx_branches, *lat_branches, wg, bg, wb, wo)


def _ffn_kernel(x_ref, mod_ref, g_ref, wi_ref, wo_ref, fg_ref, *o_refs, final):
    ys = []
    for r in range(ROW_SPLIT):
        rows = slice(r * TOKEN_TILE // ROW_SPLIT, (r + 1) * TOKEN_TILE // ROW_SPLIT)
        x = x_ref[rows, :]
        h = _rms(x, g_ref[...]) * (1.0 + mod_ref[4:5, :]) + mod_ref[3:4, :]
        hb = h.astype(BF16)
        a = _dot(hb, wi_ref[:, :D_FF])
        b = _dot(hb, wi_ref[:, D_FF:])
        f = (a * _sigmoid(a)) * b
        y = x + mod_ref[5:6, :] * _dot(f.astype(BF16), wo_ref[...])
        if not final:
            o_refs[0][rows, :] = y
        else:
            ys.append(_rms(y, fg_ref[...]))
    if not final:
        return
    y = jnp.concatenate(ys, axis=0)
    ctx_ref, lat_ref = o_refs

    @pl.when(_is_ctx_tile())
    def _():
        ctx_ref[...] = y

    @pl.when(jnp.logical_not(_is_ctx_tile()))
    def _():
        lat_ref[...] = y


def _ffn(x, mod, layer, g, wi, wo, fg, final):
    if final:
        out_specs = [_ctx_tile_spec(D_MODEL), _lat_tile_spec(D_MODEL)]
        out_shape = [jax.ShapeDtypeStruct((N_CTX_TOK, D_MODEL), F32),
                     jax.ShapeDtypeStruct((N_LAT_TOK, D_MODEL), F32)]
    else:
        out_specs = _tok_spec(D_MODEL)
        out_shape = jax.ShapeDtypeStruct((N_TOK, D_MODEL), F32)
    return pl.pallas_call(
        functools.partial(_ffn_kernel, final=final),
        grid=(N_TOK // TOKEN_TILE,),
        in_specs=[_tok_spec(D_MODEL), _mod_spec(layer), _const_spec((1, D_MODEL)),
                  _const_spec((D_MODEL, 2 * D_FF)), _const_spec((D_FF, D_MODEL)),
                  _const_spec((1, D_MODEL))],
        out_specs=out_specs,
        out_shape=out_shape,
        compiler_params=_params(("arbitrary",), VMEM_LIMIT),
        name="ffn",
    )(x, mod, g, wi, wo, fg)


@functools.lru_cache(maxsize=None)
def _dft_tables(n, blocks):
    k = np.arange(n, dtype=np.int64)
    ang = 2.0 * np.pi * ((k[:, None] * k[None, :]) % n).astype(np.float64) / n
    out = []
    for m in (np.cos(ang), np.sin(ang)):
        m = np.kron(np.eye(blocks), m / math.sqrt(n)).astype(np.float32)
        hi = m.astype(BF16)
        lo = (m - hi.astype(np.float32)).astype(BF16)
        out += [hi, lo]
    return tuple(out)


def _fourier_direct(u, pos_refs, chan_refs):
    ch_ref, cl_ref, sh_ref, sl_ref = pos_refs
    cch_ref, ccl_ref, sch_ref, scl_ref = chan_refs
    u_hi, u_lo = _split(u)
    a_hi, a_lo = _split(_dot3(ch_ref[...], cl_ref[...], u_hi, u_lo))
    b_hi, b_lo = _split(_dot3(sh_ref[...], sl_ref[...], u_hi, u_lo))
    return _dot3(a_hi, a_lo, cch_ref[...], ccl_ref[...]) - _dot3(b_hi, b_lo, sch_ref[...], scl_ref[...])


FFT_RADIX = 8
FFT_INNER = DEC_SEQ // FFT_RADIX


@functools.lru_cache(maxsize=None)
def _twiddle_tables():
    r = np.arange(FFT_RADIX, dtype=np.float64)[:, None]
    f2 = np.arange(FFT_INNER, dtype=np.float64)[None, :]
    ang = 2.0 * np.pi * r * f2 / DEC_SEQ
    scale = 1.0 / math.sqrt(FFT_RADIX)
    shape = (FFT_RADIX, FFT_INNER, 256)
    return tuple(np.ascontiguousarray(np.broadcast_to((f(ang) * scale)[:, :, None], shape)).astype(np.float32)
                 for f in (np.cos, np.sin))


def _fourier_latent_kernel(u0_ref, u1_ref, ch_ref, cl_ref, sh_ref, sl_ref, twc_ref, tws_ref, cch_ref, ccl_ref,
                           sch_ref, scl_ref, o_ref, tre_ref, tim_ref):
    for r in range(FFT_RADIX):
        rows = pl.ds(r, FFT_INNER, stride=FFT_RADIX)
        x_hi, x_lo = _split(jnp.concatenate([u0_ref[rows, :], u1_ref[rows, :]], axis=1))
        g_re = _dot3(ch_ref[...], cl_ref[...], x_hi, x_lo)
        g_im = -_dot3(sh_ref[...], sl_ref[...], x_hi, x_lo)
        wc, ws = twc_ref[r], tws_ref[r]
        tre_ref[r] = g_re * wc + g_im * ws
        tim_ref[r] = g_im * wc - g_re * ws

    def axpy(acc, coef, x):
        if abs(coef) < 1e-9:
            return acc
        term = x if abs(coef - 1.0) < 1e-9 else (-x if abs(coef + 1.0) < 1e-9 else coef * x)
        return term if acc is None else acc + term

    for f1 in range(FFT_RADIX):
        p_re = p_im = None
        for r in range(FFT_RADIX):
            ang = 2.0 * math.pi * ((f1 * r) % FFT_RADIX) / FFT_RADIX
            a, b = math.cos(ang), math.sin(ang)
            t_re, t_im = tre_ref[r], tim_ref[r]
            p_re = axpy(axpy(p_re, a, t_re), b, t_im)
            p_im = axpy(axpy(p_im, a, t_im), -b, t_re)
        re_hi, re_lo = _split(p_re)
        im_hi, im_lo = _split(p_im)
        y = _dot3(re_hi, re_lo, cch_ref[...], ccl_ref[...]) + _dot3(im_hi, im_lo, sch_ref[...], scl_ref[...])
        o_ref[f1 * FFT_INNER:(f1 + 1) * FFT_INNER, :] = y.astype(o_ref.dtype)


def _fourier_latent(p):
    width = 256
    first_blk = N_CTX_TOK // DEC_SEQ
    inner = [jnp.asarray(t) for t in _dft_tables(FFT_INNER, 1)]
    chan = [jnp.asarray(t) for t in _dft_tables(width // 4, 4)]
    tw = [jnp.asarray(t) for t in _twiddle_tables()]

    def u_spec(half):
        return pl.BlockSpec((DEC_SEQ, LANES), lambda b: (first_blk + b, FNET_U // LANES + half))

    def table(shape):
        nd = len(shape)
        return pl.BlockSpec(shape, lambda b: (0,) * nd, pipeline_mode=pl.Buffered(1))

    return pl.pallas_call(
        _fourier_latent_kernel,
        grid=(DEC_BATCH,),
        in_specs=[u_spec(0), u_spec(1)] + [table((FFT_INNER, FFT_INNER))] * 4
                 + [table((FFT_RADIX, FFT_INNER, width))] * 2 + [table((width, width))] * 4,
        out_specs=pl.BlockSpec((DEC_SEQ, width), lambda b: (b, 0)),
        out_shape=jax.ShapeDtypeStruct((N_LAT_TOK, width), BF16),
        scratch_shapes=[pltpu.VMEM((FFT_RADIX, FFT_INNER, width), F32)] * 2,
        compiler_params=_params(("arbitrary",), VMEM_LIMIT),
        name="fourier_lat",
    )(p, p, *inner, *tw, *chan)


def _ctx_mixer_kernel(p_ref, sink_ref, lp_ref, sg_ref, *refs, lam_init):
    pos_refs, chan_refs = refs[:4], refs[4:8]
    out_refs = refs[8:]
    for s in range(CTX_SEQS):
        rows = slice(s * SEQ, (s + 1) * SEQ)
        _ctx_mix_sequence(p_ref.at[rows, :], sink_ref, lp_ref, sg_ref, pos_refs, chan_refs,
                          [o.at[rows, :] for o in out_refs], lam_init)


def _ctx_mix_sequence(p_ref, sink_ref, lp_ref, sg_ref, pos_refs, chan_refs, out_refs, lam_init):
    na_ref, swa_ref, fnet_ref, diff_ref = out_refs
    fnet_ref[...] = _fourier_direct(p_ref[:, FNET_U:FNET_U + 256], pos_refs, chan_refs).astype(fnet_ref.dtype)

    shape = (SEQ, LANES)
    lane = _lane(shape)
    half = lane // HEAD_DIM
    quarter = lane // DIFF_QK_DIM
    scale = HEAD_DIM ** -0.5 * LOG2E

    def pair(col, j):
        return p_ref[:, col + LANES * j: col + LANES * (j + 1)]

    for j in range(2):
        q2 = pair(NA_Q, j) * scale
        kb = pair(NA_K, j).astype(BF16)
        vb = pair(NA_V, j).astype(BF16)
        outs = []
        for g in range(2):
            qm = jnp.where(half == g, q2, 0.0).astype(BF16)
            (e,), l = _exp_parts([_dot_nt(qm, kb)])
            outs.append(_dot(e.astype(BF16), vb) / l)
        na_ref[:, LANES * j: LANES * (j + 1)] = jnp.where(half == 0, outs[0], outs[1]).astype(na_ref.dtype)

    k2 = p_ref[:, SWA_K: SWA_K + LANES]
    v2 = p_ref[:, SWA_V: SWA_V + LANES]
    k2s = pltpu.roll(k2, HEAD_DIM, 1)
    v2s = pltpu.roll(v2, HEAD_DIM, 1)
    for j in range(2):
        kb = jnp.where(half == j, k2, k2s).astype(BF16)
        vb = jnp.where(half == j, v2, v2s).astype(BF16)
        q2 = pair(SWA_Q, j) * scale
        outs = []
        for g in range(2):
            qm = jnp.where(half == g, q2, 0.0).astype(BF16)
            (e,), l = _exp_parts([_dot_nt(qm, kb)], extra=sink_ref[2 * j + g] * LOG2E)
            outs.append(_dot(e.astype(BF16), vb) / l)
        swa_ref[:, LANES * j: LANES * (j + 1)] = jnp.where(half == 0, outs[0], outs[1]).astype(swa_ref.dtype)

    lam = _lambda(lp_ref[...], lam_init)
    for j in range(2):
        q2 = pair(DIFF_Q, j) * (DIFF_QK_DIM ** -0.5 * LOG2E)
        kb = pair(DIFF_K, j).astype(BF16)
        vb = pair(DIFF_V, j).astype(BF16)
        outs = []
        for g in range(2):
            o = []
            for c in range(2):
                qm = jnp.where(quarter == 2 * g + c, q2, 0.0).astype(BF16)
                (e,), l = _exp_parts([_dot_nt(qm, kb)])
                o.append(_dot(e.astype(BF16), vb) / l)
            outs.append(o[0] - lam * o[1])
        o2 = jnp.where(half == 0, outs[0], outs[1])
        diff_ref[:, LANES * j: LANES * (j + 1)] = _subln(o2, sg_ref[...], lam_init).astype(diff_ref.dtype)


def _ctx_mixers(p, sink, lp, sg2, lam_init):
    out = jax.ShapeDtypeStruct((N_CTX_TOK, 256), BF16)
    ospec = pl.BlockSpec((CTX_SEQS * SEQ, 256), lambda b: (b, 0))
    tables = [jnp.asarray(t) for t in _dft_tables(SEQ, 1) + _dft_tables(256 // 4, 4)]
    table_spec = pl.BlockSpec((256, 256), lambda b: (0, 0), pipeline_mode=pl.Buffered(1))
    return pl.pallas_call(
        functools.partial(_ctx_mixer_kernel, lam_init=lam_init),
        grid=(BATCH // CTX_SEQS,),
        in_specs=[pl.BlockSpec((CTX_SEQS * SEQ, D_IN), lambda b: (b, 0)),
                  pl.BlockSpec(memory_space=pltpu.SMEM),
                  pl.BlockSpec((4, DIFF_QK_DIM), lambda b: (0, 0)),
                  pl.BlockSpec((1, LANES), lambda b: (0, 0))] + [table_spec] * 8,
        out_specs=[ospec] * 4,
        out_shape=[out] * 4,
        compiler_params=_params(("arbitrary",), VMEM_LIMIT),
        name="ctx_mixers",
    )(p, sink, lp, sg2, *tables)


N_DR = 2 * NA_WIN_H - 1
N_DC = 2 * NA_WIN_W - 1
N_PAIR_BLOCKS = N_DR + 1


def _na_band_start(t):
    return jnp.clip(NA_TILE_ROWS * t - NA_WIN_H // 2, 0, GRID_H - NA_BAND_ROWS)


def _na_bias_rows(rpb):
    rows = jnp.pad(rpb, ((0, 0), (1, 1), (0, GRID_W - N_DC)))
    return jnp.concatenate([rows[:, :N_PAIR_BLOCKS], rows[:, 1:]], axis=-1)


def _lat_na_kernel(q_ref, k_ref, v_ref, ck_ref, cv_ref, rows_ref, o_ref, pair_ref, keys_ref, vals_ref):
    step = pl.program_id(2)
    blk = (GRID_W, LANES)

    @pl.when(step == 0)
    def _():
        c = lax.broadcasted_iota(jnp.int32, blk, 0)
        kc = _lane(blk) % GRID_W
        cs = jnp.clip(c - NA_WIN_W // 2, 0, GRID_W - NA_WIN_W)
        in_cols = (kc >= cs) & (kc < cs + NA_WIN_W)
        for g in range(2):
            for i in range(N_PAIR_BLOCKS):
                row = jnp.broadcast_to(rows_ref[g, i:i + 1, :], blk) * LOG2E
                toeplitz = pltpu.roll(row, LANES - (NA_WIN_W - 1), 1, stride=1, stride_axis=0)
                pair_ref[g, i] = jnp.where(in_cols, toeplitz, NEG_INF)

        _stage_keys_values(keys_ref, vals_ref, k_ref[...], v_ref[...], ck_ref[...], cv_ref[...])

    def bias(t, band, g):
        rows = []
        for ri in range(NA_TILE_ROWS):
            r = NA_TILE_ROWS * t + ri
            first = jnp.clip(r - NA_WIN_H // 2, 0, GRID_H - NA_WIN_H) - band
            d0 = band - r + NA_WIN_H - 1
            blocks = []
            for m in range(NA_BAND_ROWS // 2):
                idx = jnp.clip(d0 + 2 * m + 1, 0, N_PAIR_BLOCKS - 1)
                jrow = 2 * m + _lane(blk) // GRID_W
                in_rows = (jrow >= first) & (jrow < first + NA_WIN_H)
                blocks.append(jnp.where(in_rows, pair_ref[g, idx], NEG_INF))
            rows.append(jnp.concatenate(blocks, axis=1))
        return jnp.concatenate(rows, axis=0)

    tiles = []
    for u in range(MIX_TILES):
        t = MIX_TILES * step + u
        band = _na_band_start(t)
        q2 = q_ref[u * Q_TILE:(u + 1) * Q_TILE, :] * (HEAD_DIM ** -0.5 * LOG2E)
        tiles.append((q2, pl.multiple_of(band * GRID_W, GRID_W),
                      lambda g, s, t=t, band=band: s + bias(t, band, g)))
    outs = _attend_tiles(tiles, keys_ref, vals_ref, NA_BAND_ROWS * GRID_W)
    for u, out in enumerate(outs):
        o_ref[u * Q_TILE:(u + 1) * Q_TILE, :] = out.astype(o_ref.dtype)


def _stage_keys_values(keys_ref, vals_ref, k2, v2, ck2, cv2):
    keys_ref[:DEC_SEQ, :] = k2.astype(BF16)
    keys_ref[DEC_SEQ:, :] = ck2.astype(BF16)
    for g in range(2):
        vals_ref[g, :DEC_SEQ, :] = jnp.where(_lane(v2.shape) // HEAD_DIM == g, v2, 1.0).astype(BF16)
        vals_ref[g, DEC_SEQ:, :] = jnp.where(_lane(cv2.shape) // HEAD_DIM == g, cv2, 1.0).astype(BF16)


def _attend_tiles(tiles, keys_ref, vals_ref, n_loc, sink=None):
    ctx = slice(DEC_SEQ, DEC_SEQ + PAST_LEN)
    jobs = []
    for q2, start, fix_local in tiles:
        half = _lane(q2.shape) // HEAD_DIM
        loc = pl.ds(start, n_loc)
        for g in range(2):
            qm = jnp.where(half == g, q2, 0.0).astype(BF16)
            jobs.append((g, half, loc, fix_local(g, _dot_nt(qm, keys_ref[loc, :])), _dot_nt(qm, keys_ref[ctx, :])))
    outs = []
    for g, half, loc, s_loc, s_ctx in jobs:
        m = jnp.maximum(jnp.max(s_loc, axis=-1, keepdims=True), jnp.max(s_ctx, axis=-1, keepdims=True))
        if sink is not None:
            m = jnp.maximum(m, sink[g])
        acc = (_dot(jnp.exp2(s_loc - m).astype(BF16), vals_ref[g, loc, :])
               + _dot(jnp.exp2(s_ctx - m).astype(BF16), vals_ref[g, ctx, :]))
        if sink is not None:
            acc = acc + jnp.where(half == g, 0.0, jnp.exp2(sink[g] - m))
        outs.append(acc / jnp.where(half == g, pltpu.roll(acc, HEAD_DIM, 1), 1.0))
    half = _lane(outs[0].shape) // HEAD_DIM
    return [jnp.where(half == 0, outs[2 * u], outs[2 * u + 1]) for u in range(len(tiles))]


def _staging_scratch():
    n = DEC_SEQ + PAST_LEN
    return [pltpu.VMEM((n, LANES), BF16), pltpu.VMEM((2, n, LANES), BF16)]


def _lat_blocks(col, tiles=1):
    rows = tiles * Q_TILE
    qt = DEC_SEQ // rows
    first_q = N_CTX_TOK // rows
    first_k = N_CTX_TOK // DEC_SEQ
    q = pl.BlockSpec((rows, LANES), lambda b, j, t: (first_q + qt * b + t, col[0] // LANES + j))
    k = pl.BlockSpec((DEC_SEQ, LANES), lambda b, j, t: (first_k + b, col[1] // LANES + j))
    v = pl.BlockSpec((DEC_SEQ, LANES), lambda b, j, t: (first_k + b, col[2] // LANES + j))
    o = pl.BlockSpec((rows, LANES), lambda b, j, t: (qt * b + t, j))
    return q, k, v, o


def _cache_spec(layer, which, shared_kv):
    return pl.BlockSpec((None, None, None, PAST_LEN, LANES),
                        lambda b, j, t: (b, layer, which, 0, 0 if shared_kv else j))


def _lat_na(p, cache, bias_rows, layer):
    q, k, v, o = _lat_blocks((NA_Q, NA_K, NA_V), MIX_TILES)
    rows_spec = pl.BlockSpec((2, N_PAIR_BLOCKS, LANES), lambda b, j, t: (j, 0, 0))
    return pl.pallas_call(
        _lat_na_kernel,
        grid=(DEC_BATCH, 2, DEC_SEQ // (MIX_TILES * Q_TILE)),
        in_specs=[q, k, v, _cache_spec(layer, 0, False), _cache_spec(layer, 1, False), rows_spec],
        out_specs=o,
        out_shape=jax.ShapeDtypeStruct((N_LAT_TOK, 256), BF16),
        scratch_shapes=[pltpu.VMEM((2, N_PAIR_BLOCKS, GRID_W, LANES), F32)] + _staging_scratch(),
        compiler_params=_params(("arbitrary",) * 3, VMEM_LIMIT),
        name="lat_na",
    )(p, p, p, cache, cache, bias_rows)


@functools.lru_cache(maxsize=None)
def _rope_tables(dim):
    quarter = dim // 4
    pos = np.arange(DEC_SEQ)
    rows, cols = pos // GRID_W, pos % GRID_W
    lane = np.arange(LANES)
    w = lane % dim
    axis_pos = np.where((w // (dim // 2) == 0)[None, :], rows[:, None], cols[:, None]).astype(np.float64)
    u = w % (dim // 2)
    inv = ROPE_BASE ** (-(u % quarter).astype(np.float64) * 2.0 / (dim // 2))
    ang = axis_pos * inv[None, :]
    sign = np.where(u < quarter, -1.0, 1.0)[None, :]
    return np.cos(ang).astype(np.float32), (np.sin(ang) * sign).astype(np.float32)


def _swa_key_start(t):
    return jnp.clip(Q_TILE * t - SWA_WINDOW, 0, DEC_SEQ - SWA_KEYS)


def _lat_swa_kernel(q_ref, k_ref, v_ref, ck_ref, cv_ref, cos_ref, sin_ref, sink_ref, o_ref, keys_ref, vals_ref):
    j = pl.program_id(1)
    step = pl.program_id(2)
    quarter = HEAD_DIM // 4

    @pl.when(step == 0)
    def _():
        def head_j(x):
            return jnp.where(_lane(x.shape) // HEAD_DIM == j, x, pltpu.roll(x, HEAD_DIM, 1))

        _stage_keys_values(keys_ref, vals_ref, head_j(_rope(k_ref[...], cos_ref[...], sin_ref[...], quarter)),
                           head_j(v_ref[...]), head_j(ck_ref[...]), head_j(cv_ref[...]))

    tiles = []
    for u in range(MIX_TILES):
        t = MIX_TILES * step + u
        q0 = pl.multiple_of(t * Q_TILE, Q_TILE)
        k0 = pl.multiple_of(_swa_key_start(t), SWA_WINDOW)
        q2 = _rope(q_ref[u * Q_TILE:(u + 1) * Q_TILE, :], cos_ref[pl.ds(q0, Q_TILE), :],
                   sin_ref[pl.ds(q0, Q_TILE), :], quarter)
        qpos = q0 + lax.broadcasted_iota(jnp.int32, (Q_TILE, SWA_KEYS), 0)
        kpos = k0 + lax.broadcasted_iota(jnp.int32, (Q_TILE, SWA_KEYS), 1)
        valid = jnp.abs(kpos - qpos) <= SWA_WINDOW
        tiles.append((q2 * (HEAD_DIM ** -0.5 * LOG2E), k0,
                      lambda g, s, valid=valid: jnp.where(valid, s, NEG_INF)))
    sink = [sink_ref[2 * j + g] * LOG2E for g in range(2)]
    outs = _attend_tiles(tiles, keys_ref, vals_ref, SWA_KEYS, sink)
    for u, out in enumerate(outs):
        o_ref[u * Q_TILE:(u + 1) * Q_TILE, :] = out.astype(o_ref.dtype)


def _lat_swa(p, cache, sink, layer):
    q, _, _, o = _lat_blocks((SWA_Q, SWA_K, SWA_V), MIX_TILES)
    first_k = N_CTX_TOK // DEC_SEQ
    k = pl.BlockSpec((DEC_SEQ, LANES), lambda b, j, t: (first_k + b, SWA_K // LANES))
    v = pl.BlockSpec((DEC_SEQ, LANES), lambda b, j, t: (first_k + b, SWA_V // LANES))
    cos, sin = (jnp.asarray(a) for a in _rope_tables(HEAD_DIM))
    tab = pl.BlockSpec((DEC_SEQ, LANES), lambda b, j, t: (0, 0))
    return pl.pallas_call(
        _lat_swa_kernel,
        grid=(DEC_BATCH, 2, DEC_SEQ // (MIX_TILES * Q_TILE)),
        in_specs=[q, k, v, _cache_spec(layer, 0, True), _cache_spec(layer, 1, True), tab, tab,
                  pl.BlockSpec(memory_space=pltpu.SMEM)],
        out_specs=o,
        out_shape=jax.ShapeDtypeStruct((N_LAT_TOK, 256), BF16),
        scratch_shapes=_staging_scratch(),
        compiler_params=_params(("arbitrary",) * 3, VMEM_LIMIT),
        name="lat_swa",
    )(p, p, p, cache, cache, cos, sin, sink)


DIFF_KEYS = DEC_SEQ + PAST_LEN
DIFF_TILES = 2
KEY_CHUNK = 256


def _lat_diff_kernel(*refs, lam_init, n_cast):
    q_ref, k_ref, v_ref, ck_ref, cv_ref, cos_ref, sin_ref, lp_ref, sg_ref = refs[:9]
    w_refs = refs[9:9 + n_cast]
    o_ref = refs[9 + n_cast]
    wb_refs = refs[10 + n_cast:10 + 2 * n_cast]
    keys_ref, vals_ref, s_ref = refs[10 + 2 * n_cast:]
    step = pl.program_id(2)
    eighth = DIFF_QK_DIM // 4

    for w_ref, wb_ref in zip(w_refs, wb_refs):
        wb_ref[...] = w_ref[...].astype(BF16)

    @pl.when(step == 0)
    def _():
        _stage_keys_values(keys_ref, vals_ref, _rope(k_ref[...], cos_ref[...], sin_ref[...], eighth), v_ref[...],
                           ck_ref[...], cv_ref[...])

    lam = _lambda(lp_ref[...], lam_init)
    lane = _lane((Q_TILE, LANES))
    quarter = lane // DIFF_QK_DIM
    half = lane // HEAD_DIM
    n_chunks = DIFF_KEYS // KEY_CHUNK

    q2s = []
    for u in range(DIFF_TILES):
        q0 = pl.multiple_of((DIFF_TILES * step + u) * Q_TILE, Q_TILE)
        q2 = _rope(q_ref[u * Q_TILE:(u + 1) * Q_TILE, :], cos_ref[pl.ds(q0, Q_TILE), :],
                   sin_ref[pl.ds(q0, Q_TILE), :], eighth)
        q2s.append(q2 * (DIFF_QK_DIM ** -0.5 * LOG2E))

    def scores(n_map):
        u, i = divmod(n_map, 4)
        qm = jnp.where(quarter == i, q2s[u], 0.0).astype(BF16)
        m_run = None
        for n in range(n_chunks):
            cols = slice(n * KEY_CHUNK, (n + 1) * KEY_CHUNK)
            s = _dot_nt(qm, keys_ref[cols, :])
            s_ref[n_map, :, cols] = s
            m_blk = jnp.maximum(s[:, :LANES], s[:, LANES:])
            m_run = m_blk if m_run is None else jnp.maximum(m_run, m_blk)
        return jnp.max(m_run, axis=-1, keepdims=True)

    def values(n_map, m):
        g = (n_map % 4) // 2
        acc = None
        for n in range(n_chunks):
            cols = slice(n * KEY_CHUNK, (n + 1) * KEY_CHUNK)
            part = _dot(jnp.exp2(s_ref[n_map, :, cols] - m).astype(BF16), vals_ref[g, cols, :])
            acc = part if acc is None else acc + part
        return acc / jnp.where(half == g, pltpu.roll(acc, HEAD_DIM, 1), 1.0)

    n_maps = 4 * DIFF_TILES
    o = []
    m_next = scores(0)
    for n_map in range(n_maps):
        m = m_next
        if n_map + 1 < n_maps:
            m_next = scores(n_map + 1)
        o.append(values(n_map, m))
    for u in range(DIFF_TILES):
        o0, o1, o2, o3 = o[4 * u:4 * u + 4]
        pair = jnp.where(half == 0, o0 - lam * o1, o2 - lam * o3)
        o_ref[u * Q_TILE:(u + 1) * Q_TILE, :] = _subln(pair, sg_ref[...], lam_init).astype(o_ref.dtype)


def _lat_diff(p, cache, lp, sg2, layer, lam_init, casts):
    q, k, v, o = _lat_blocks((DIFF_Q, DIFF_K, DIFF_V), DIFF_TILES)
    cos, sin = (jnp.asarray(a) for a in _rope_tables(DIFF_QK_DIM))
    tab = pl.BlockSpec((DEC_SEQ, LANES), lambda b, j, t: (0, 0))
    n_q = DEC_SEQ // (DIFF_TILES * Q_TILE)
    n_steps = DEC_BATCH * 2 * n_q
    cast_in, cast_out, cast_shape = [], [], []
    for w, w_layer, block_rows in casts:
        _, rows, cols = w.shape
        per_block = n_steps // (rows // block_rows)

        def block(b, j, t, per_block=per_block):
            return ((b * 2 + j) * n_q + t) // per_block

        cast_in.append(pl.BlockSpec((None, block_rows, cols),
                                    lambda b, j, t, w_layer=w_layer, block=block: (w_layer, block(b, j, t), 0)))
        cast_out.append(pl.BlockSpec((block_rows, cols), lambda b, j, t, block=block: (block(b, j, t), 0)))
        cast_shape.append(jax.ShapeDtypeStruct((rows, cols), BF16))
    return pl.pallas_call(
        functools.partial(_lat_diff_kernel, lam_init=lam_init, n_cast=len(casts)),
        grid=(DEC_BATCH, 2, n_q),
        in_specs=[q, k, v, _cache_spec(layer, 0, False), _cache_spec(layer, 1, False), tab, tab,
                  pl.BlockSpec((4, DIFF_QK_DIM), lambda b, j, t: (0, 0)),
                  pl.BlockSpec((1, LANES), lambda b, j, t: (0, 0))] + cast_in,
        out_specs=[o] + cast_out,
        out_shape=[jax.ShapeDtypeStruct((N_LAT_TOK, 256), BF16)] + cast_shape,
        scratch_shapes=_staging_scratch() + [pltpu.VMEM((4 * DIFF_TILES, Q_TILE, DIFF_KEYS), F32)],
        compiler_params=_params(("arbitrary",) * 3, VMEM_LIMIT),
        name="lat_diff",
    )(p, p, p, cache, cache, cos, sin, lp, sg2, *[w for w, _, _ in casts])


def kernel(x_prompt, x_sample, cache_na_kv, cache_swa_kv, cache_diff_kv, c, c_ctx, norm1_g, norm2_g, ada_w,
           ada_b, w_in, na_rpb, swa_sink, diff_lambda, diff_subln_g, w_branch, w_gate, b_gate, w_o, w_ffn_in,
           w_ffn_out, final_norm_g):
    x = (x_prompt.reshape(N_CTX_TOK, D_MODEL), x_sample.reshape(N_LAT_TOK, D_MODEL))
    cond = jnp.zeros((COND_ROWS, D_MODEL), F32).at[0].set(c_ctx).at[1:1 + DEC_BATCH].set(c)
    mod = _adaln(cond, ada_w, ada_b).reshape(DEPTH, COND_ROWS, 6, D_MODEL)

    cache_na = cache_na_kv.reshape(DEC_BATCH, DEPTH, 2, PAST_LEN, 256)
    cache_swa = cache_swa_kv.reshape(DEC_BATCH, DEPTH, 2, PAST_LEN, LANES)
    cache_diff = cache_diff_kv.reshape(DEC_BATCH, DEPTH, 2, PAST_LEN, 256)
    fg = final_norm_g.reshape(1, D_MODEL)

    states = None
    w_in_b = w_in[0].astype(BF16)
    w_branch2 = w_branch.reshape(DEPTH, D_MODEL, D_MODEL)
    for l in range(DEPTH):
        lam_init = 0.8 - 0.6 * math.exp(-0.3 * l)
        g1 = norm1_g[l].reshape(1, D_MODEL)
        sg2 = jnp.tile(diff_subln_g[l], 2).reshape(1, LANES)
        p, h, *states = _proj(x, mod, l, g1, w_in_b, states)

        c_na, c_swa, c_f, c_diff = _ctx_mixers(p, swa_sink[l], diff_lambda[l], sg2, lam_init)
        l_na = _lat_na(p, cache_na, _na_bias_rows(na_rpb[l]), l)
        l_swa = _lat_swa(p, cache_swa, swa_sink[l], l)
        l_f = _fourier_latent(p)
        casts = [(w_gate, l, 32), (w_branch2, l, 32), (w_o, l, 32), (w_ffn_in, l, 32), (w_ffn_out, l, 176)]
        if l + 1 < DEPTH:
            casts.append((w_in, l + 1, 32))
        l_diff, wg_b, wb_b, wo_b, wi_b, wout_b, *nxt = _lat_diff(p, cache_diff, diff_lambda[l], sg2, l, lam_init,
                                                                  casts)
        if nxt:
            w_in_b = nxt[0]

        x = _merge(x, mod, l, h, (c_na, c_swa, c_f, c_diff), (l_na, l_swa, l_f, l_diff),
                   wg_b, b_gate[l].reshape(1, -1), wb_b.reshape(N_BRANCH, BRANCH_DIM, D_MODEL), wo_b)
        x = _ffn(x, mod, l, norm2_g[l].reshape(1, D_MODEL), wi_b, wout_b, fg, final=(l == DEPTH - 1))

    y_ctx, y_lat = x
    na_kv, swa_kv, diff_kv = states
    return (y_ctx.reshape(BATCH, SEQ, D_MODEL), y_lat.reshape(DEC_BATCH, DEC_SEQ, D_MODEL),
            na_kv.reshape(BATCH, DEPTH, 2, SEQ, 4, HEAD_DIM), swa_kv.reshape(BATCH, DEPTH, 2, SEQ, 2, HEAD_DIM),
            diff_kv.reshape(BATCH, DEPTH, 2, SEQ, 4, HEAD_DIM))
```
